```python
import math
import jax, jax.numpy as jnp
from jax import lax
import numpy as np

D_MODEL = 4096
BATCH = 4
SEQ = 4096
DEPTH = 4
DEC_BATCH = 2
DEC_SEQ = 8192
PAST_LEN = 128

HEAD_DIM = 128
ML_W = D_MODEL // 4
NA_W = 3 * D_MODEL // 8
DF_W = D_MODEL - ML_W - NA_W
ML_HEADS = ML_W // HEAD_DIM
NA_HEADS = NA_W // HEAD_DIM
DF_HEADS = DF_W // HEAD_DIM
DF_HALF = HEAD_DIM // 2
ROPE_DIM = DF_HALF // 4
ROPE_THETA = 500000.0
ML_CHUNK = 64
GRID_W = 64
NA_WIN_R = 8
NA_WIN_C = 16
Q_BLOCK = 128
N_EXPERTS = 32
N_GROUPS = 8
EXPERTS_PER_GROUP = N_EXPERTS // N_GROUPS
TOP_K = 2
D_FF = D_MODEL // 2
MOE_BLOCK = 256
ALPHA = (2 * DEPTH) ** 0.25
BETA = (8 * DEPTH) ** -0.25
LN_EPS = 1e-5
IN_COL_SIZES = (ML_W,) * 4 + (4 * ML_HEADS,) + (NA_W,) * 3 + (DF_W,) * 3
V_COL_GROUPS = (2, 7, 10)
IN_W = sum(IN_COL_SIZES)

kernel_name = 'hybrid_mlstm_natten_diffattn_groupmoe_encoder'


def _layer_norm(x, g, b):
    xf = x.astype(jnp.float32)
    mu = xf.mean(-1, keepdims=True)
    var = jnp.square(xf - mu).mean(-1, keepdims=True)
    return ((xf - mu) * lax.rsqrt(var + LN_EPS) * g + b).astype(x.dtype)


def _split_cols(p):
    offs, acc = [], 0
    for s in IN_COL_SIZES[:-1]:
        acc += s
        offs.append(acc)
    return jnp.split(p, offs, axis=-1)


def _heads(t, h):
    b, s, _ = t.shape
    return t.reshape(b, s, h, -1).transpose(0, 2, 1, 3)


def _merge(t):
    b, h, s, d = t.shape
    return t.transpose(0, 2, 1, 3).reshape(b, s, h * d)


def _mlstm_dir(q, k, v, ig, fg):
    B, H, T, d = q.shape
    L = ML_CHUNK
    NC = T // L
    qc = q.reshape(B, H, NC, L, d)
    kc = k.reshape(B, H, NC, L, d) * (d ** -0.5)
    vc = v.reshape(B, H, NC, L, d)
    ii = ig.reshape(B, H, NC, L)
    b = jnp.cumsum(jax.nn.log_sigmoid(fg).reshape(B, H, NC, L), axis=-1)
    g = b[..., -1]
    a = g[..., None] - b + ii
    m_loc = a.max(-1)
    w_loc = jnp.exp(a - m_loc[..., None])
    c_loc = jnp.einsum('bhcl,bhcld,bhcle->bhcde', w_loc, kc, vc)
    n_loc = jnp.einsum('bhcl,bhcld->bhcd', w_loc, kc)

    def step(carry, xs):
        c, n, m = carry
        q_i, c_i, n_i, m_i, g_i = xs
        qC = jnp.einsum('bhld,bhde->bhle', q_i, c)
        qn = jnp.einsum('bhld,bhd->bhl', q_i, n)
        m_new = jnp.maximum(g_i + m, m_i)
        s_old = jnp.exp(g_i + m - m_new)
        s_new = jnp.exp(m_i - m_new)
        c = s_old[..., None, None] * c + s_new[..., None, None] * c_i
        n = s_old[..., None] * n + s_new[..., None] * n_i
        return (c, n, m_new), (qC, qn, m)

    init = (jnp.zeros((B, H, d, d), jnp.float32), jnp.zeros((B, H, d), jnp.float32),
            jnp.zeros((B, H), jnp.float32))
    xs = tuple(jnp.moveaxis(t, 2, 0) for t in (qc, c_loc, n_loc, m_loc, g))
    _, (qC, qn, m_prev) = lax.scan(step, init, xs)
    qC = jnp.moveaxis(qC, 0, 2)
    qn = jnp.moveaxis(qn, 0, 2)
    m_prev = jnp.moveaxis(m_prev, 0, 2)

    lower = jnp.tril(jnp.ones((L, L), bool))
    dmat = jnp.where(lower, b[..., :, None] - b[..., None, :] + ii[..., None, :], -jnp.inf)
    m_inter = b + m_prev[..., None]
    m_row = jnp.maximum(m_inter, dmat.max(-1))
    w_inter = jnp.exp(m_inter - m_row)
    s = jnp.exp(dmat - m_row[..., None]) * jnp.einsum('bhcjd,bhcld->bhcjl', qc, kc)
    num = w_inter[..., None] * qC + jnp.einsum('bhcjl,bhcle->bhcje', s, vc)
    den = w_inter * qn + s.sum(-1)
    h = num / jnp.maximum(jnp.abs(den), jnp.exp(-m_row))[..., None]
    return h.reshape(B, H, T, d)


def _mlstm_mixer(q, k, v, o, gates, f_bias, norm_g):
    dt = q.dtype
    B, T, _ = q.shape
    q, k, v = (_heads(t, ML_HEADS).astype(jnp.float32) for t in (q, k, v))
    ig_f, fg_f, ig_b, fg_b = gates.astype(jnp.float32).reshape(B, T, 4, ML_HEADS).transpose(2, 0, 3, 1)
    fg_f = fg_f + f_bias[0][:, None]
    fg_b = fg_b + f_bias[1][:, None]
    flip = lambda t: jnp.flip(t, axis=2)
    h_f = _mlstm_dir(q, k, v, ig_f, fg_f)
    h_b = flip(_mlstm_dir(flip(q), flip(k), flip(v), flip(ig_b), flip(fg_b)))
    h = h_f + h_b
    mu = h.mean(-1, keepdims=True)
    var = jnp.square(h - mu).mean(-1, keepdims=True)
    h = (h - mu) * lax.rsqrt(var + LN_EPS) * norm_g.reshape(ML_HEADS, HEAD_DIM)[:, None, :]
    return (_merge(h) * jax.nn.sigmoid(o.astype(jnp.float32))).astype(dt)


def _natten_mixer(q, k, v, rpb):
    dt = q.dtype
    B, T, _ = q.shape
    rows = T // GRID_W
    wr = min(NA_WIN_R, rows)
    qg = (_heads(q, NA_HEADS) * (HEAD_DIM ** -0.5)).reshape(B, NA_HEADS, rows, GRID_W, HEAD_DIM)
    kg = _heads(k, NA_HEADS).reshape(B, NA_HEADS, rows, GRID_W, HEAD_DIM)
    vg = _heads(v, NA_HEADS).reshape(B, NA_HEADS, rows, GRID_W, HEAD_DIM)
    cols = jnp.arange(GRID_W)
    cs = jnp.clip(cols - NA_WIN_C // 2, 0, GRID_W - NA_WIN_C)
    col_valid = (cols[None, :] >= cs[:, None]) & (cols[None, :] < cs[:, None] + NA_WIN_C)
    dc_idx = jnp.clip(cols[None, :] - cols[:, None], -(NA_WIN_C - 1), NA_WIN_C - 1) + NA_WIN_C - 1
    col_bias = rpb[:, :, dc_idx].astype(jnp.float32)

    def row_attn(r):
        rs = jnp.clip(r - wr // 2, 0, rows - wr)
        kr = lax.dynamic_slice_in_dim(kg, rs, wr, axis=2)
        vr = lax.dynamic_slice_in_dim(vg, rs, wr, axis=2)
        qr = lax.dynamic_index_in_dim(qg, r, axis=2, keepdims=False)
        dr_idx = rs + jnp.arange(wr) - r + NA_WIN_R - 1
        bias = jnp.take(col_bias, dr_idx, axis=1).transpose(0, 2, 1, 3)
        s = jnp.einsum('bhqd,bhrkd->bhqrk', qr, kr).astype(jnp.float32) + bias
        s = jnp.where(col_valid[:, None, :], s, -jnp.inf)
        p = jax.nn.softmax(s.reshape(B, NA_HEADS, GRID_W, wr * GRID_W), axis=-1).reshape(s.shape)
        return jnp.einsum('bhqrk,bhrkd->bhqd', p.astype(dt), vr)

    out = lax.map(row_attn, jnp.arange(rows))
    out = out.transpose(1, 2, 0, 3, 4).reshape(B, NA_HEADS, T, HEAD_DIM)
    return _merge(out)


def _rope_partial(t, pos):
    half = ROPE_DIM // 2
    inv_freq = jnp.float32(ROPE_THETA) ** (-jnp.arange(0, ROPE_DIM, 2, dtype=jnp.float32) / ROPE_DIM)
    ang = pos[:, None] * inv_freq[None, :]
    cos, sin = jnp.cos(ang), jnp.sin(ang)
    tf = t.astype(jnp.float32)
    x1 = tf[..., :half]
    x2 = tf[..., half:ROPE_DIM]
    out = jnp.concatenate([x1 * cos - x2 * sin, x2 * cos + x1 * sin, tf[..., ROPE_DIM:]], axis=-1)
    return out.astype(t.dtype)


def _diff_mixer(q, k, v, lam_p, norm_g, lambda_init):
    dt = q.dtype
    B, T, _ = q.shape
    q = _heads(q, DF_HEADS).reshape(B, DF_HEADS, T, 2, DF_HALF).transpose(0, 1, 3, 2, 4)
    k = _heads(k, DF_HEADS).reshape(B, DF_HEADS, T, 2, DF_HALF).transpose(0, 1, 3, 2, 4)
    v = _heads(v, DF_HEADS)
    pos = jnp.arange(T, dtype=jnp.float32)
    q = _rope_partial(q, pos) * (DF_HALF ** -0.5)
    k = _rope_partial(k, pos)
    lp = lam_p.astype(jnp.float32)
    lam = jnp.exp(jnp.sum(lp[0] * lp[1])) - jnp.exp(jnp.sum(lp[2] * lp[3])) + lambda_init
    nb = T // Q_BLOCK
    qb = q.reshape(B, DF_HEADS, 2, nb, Q_BLOCK, DF_HALF).transpose(3, 0, 1, 2, 4, 5)

    def block_attn(qi):
        s = jnp.einsum('bhiqd,bhikd->bhiqk', qi, k).astype(jnp.float32)
        p = jax.nn.softmax(s, axis=-1)
        a = p[:, :, 0] - lam * p[:, :, 1]
        return jnp.einsum('bhqk,bhkd->bhqd', a.astype(dt), v)

    o = lax.map(block_attn, qb)
    o = o.transpose(1, 2, 0, 3, 4).reshape(B, DF_HEADS, T, HEAD_DIM).astype(jnp.float32)
    o = o * lax.rsqrt(jnp.mean(o * o, axis=-1, keepdims=True) + LN_EPS) * norm_g * (1.0 - lambda_init)
    return _merge(o.astype(dt))


def _token_mixer(x, w_in, b_in, f_bias, ml_g, rpb, lam_p, df_g, w_out, lambda_init):
    p = x @ w_in + b_in
    mq, mk, mv, mo, mg, nq, nk, nv, dq, dk, dv = _split_cols(p)
    y_ml = _mlstm_mixer(mq, mk, mv, mo, mg, f_bias, ml_g)
    y_na = _natten_mixer(nq, nk, nv, rpb)
    y_df = _diff_mixer(dq, dk, dv, lam_p, df_g, lambda_init)
    return jnp.concatenate([y_ml, y_na, y_df], axis=-1) @ w_out


def _route(xt, w_router, b_router):
    n = xt.shape[0]
    logits = (xt @ w_router).astype(jnp.float32) + b_router.astype(jnp.float32)
    probs = jax.nn.softmax(logits, axis=-1)
    grouped = probs.reshape(n, N_GROUPS, EXPERTS_PER_GROUP)
    gscore = lax.top_k(grouped, TOP_K)[0].sum(-1)
    best = jnp.argmax(gscore, axis=-1)
    within = jnp.take_along_axis(grouped, best[:, None, None], axis=1)[:, 0]
    top_v, top_i = lax.top_k(within, TOP_K)
    idx = (best[:, None] * EXPERTS_PER_GROUP + top_i).astype(jnp.int32)
    gates = top_v / top_v.sum(-1, keepdims=True)
    return idx, gates


def _dispatch(xt, idx, gates, wg, wu, wd):
    n, d = xt.shape
    a_tot = n * TOP_K
    flat_e = idx.reshape(-1)
    flat_tok = jnp.repeat(jnp.arange(n, dtype=jnp.int32), TOP_K)
    flat_g = gates.reshape(-1)
    order = jnp.argsort(flat_e)
    se, st, sg = flat_e[order], flat_tok[order], flat_g[order]
    counts = jnp.zeros((N_EXPERTS,), jnp.int32).at[flat_e].add(1)
    padded = (counts + MOE_BLOCK - 1) // MOE_BLOCK * MOE_BLOCK
    off = jnp.cumsum(counts) - counts
    pend = jnp.cumsum(padded)
    poff = pend - padded
    dest = poff[se] + jnp.arange(a_tot, dtype=jnp.int32) - off[se]
    nb = (a_tot + N_EXPERTS * (MOE_BLOCK - 1) + MOE_BLOCK - 1) // MOE_BLOCK
    p_rows = nb * MOE_BLOCK
    row_tok = jnp.full((p_rows,), n, jnp.int32).at[dest].set(st)
    row_g = jnp.zeros((p_rows,), jnp.float32).at[dest].set(sg)
    block_e = jnp.minimum(jnp.searchsorted(pend, jnp.arange(nb) * MOE_BLOCK, side='right'),
                          N_EXPERTS - 1).astype(jnp.int32)
    x_pad = jnp.concatenate([xt, jnp.zeros((1, d), xt.dtype)], axis=0)
    xb = x_pad[row_tok].reshape(nb, MOE_BLOCK, d)

    def expert_block(args):
        xblk, e = args
        h = jax.nn.silu(xblk @ wg[e]) * (xblk @ wu[e])
        return h @ wd[e]

    yb = lax.map(expert_block, (xb, block_e)).reshape(p_rows, d)
    y = jnp.zeros((n + 1, d), xt.dtype).at[row_tok].add(yb * row_g[:, None].astype(xt.dtype))
    return y[:n]


def _moe(x, w_router, b_router, wg, wu, wd):
    B, T, d = x.shape
    xt = x.reshape(B * T, d)
    idx, gates = _route(xt, w_router, b_router)
    return _dispatch(xt, idx, gates, wg, wu, wd).reshape(B, T, d)


def _trunk(x, w_in, b_in, ml_f_bias, ml_norm_g, na_rpb, df_lambda, df_norm_g, w_out,
           ln_g, ln_b, w_router, b_router, w_gate, w_up, w_down):
    for l in range(DEPTH):
        lambda_init = 0.8 - 0.6 * math.exp(-0.3 * l)
        h = _token_mixer(x, w_in[l], b_in[l], ml_f_bias[l], ml_norm_g[l], na_rpb[l], df_lambda[l],
                         df_norm_g[l], w_out[l], lambda_init)
        x = _layer_norm(ALPHA * x + h, ln_g[l, 0], ln_b[l, 0])
        h = _moe(x, w_router, b_router, w_gate[l], w_up[l], w_down[l])
        x = _layer_norm(ALPHA * x + h, ln_g[l, 1], ln_b[l, 1])
    return x


def setup_inputs(seed: int = 0) -> dict:
    key = jax.random.key(seed)
    ks = jax.random.split(key, 17)
    f32 = jnp.float32
    nrm = lambda k, s: jax.random.normal(k, s, f32)
    col_scale = jnp.concatenate([jnp.full((s,), BETA if i in V_COL_GROUPS else 1.0, f32)
                                 for i, s in enumerate(IN_COL_SIZES)])
    return {
        'x_prompt': nrm(ks[0], (BATCH, SEQ, D_MODEL)),
        'x_sample': nrm(ks[1], (DEC_BATCH, DEC_SEQ, D_MODEL)),
        'w_in': nrm(ks[2], (DEPTH, D_MODEL, IN_W)) * (D_MODEL ** -0.5) * col_scale,
        'b_in': 0.01 * nrm(ks[3], (DEPTH, IN_W)),
        'ml_f_bias': jnp.linspace(3.0, 6.0, ML_HEADS, dtype=f32)[None, None, :]
                     + 0.1 * nrm(ks[4], (DEPTH, 2, ML_HEADS)),
        'ml_norm_g': 1.0 + 0.01 * nrm(ks[5], (DEPTH, ML_W)),
        'na_rpb': 0.02 * nrm(ks[6], (DEPTH, NA_HEADS, 2 * NA_WIN_R - 1, 2 * NA_WIN_C - 1)),
        'df_lambda': 0.1 * nrm(ks[7], (DEPTH, 4, DF_HALF)),
        'df_norm_g': 1.0 + 0.01 * nrm(ks[8], (DEPTH, HEAD_DIM)),
        'w_out': nrm(ks[9], (DEPTH, D_MODEL, D_MODEL)) * (D_MODEL ** -0.5) * BETA,
        'ln_g': 1.0 + 0.01 * nrm(ks[10], (DEPTH, 2, D_MODEL)),
        'ln_b': 0.01 * nrm(ks[11], (DEPTH, 2, D_MODEL)),
        'w_router': nrm(ks[12], (D_MODEL, N_EXPERTS)) * (D_MODEL ** -0.5),
        'b_router': 0.01 * nrm(ks[13], (N_EXPERTS,)),
        'w_gate': nrm(ks[14], (DEPTH, N_EXPERTS, D_MODEL, D_FF)) * (D_MODEL ** -0.5),
        'w_up': nrm(ks[15], (DEPTH, N_EXPERTS, D_MODEL, D_FF)) * (D_MODEL ** -0.5),
        'w_down': nrm(ks[16], (DEPTH, N_EXPERTS, D_FF, D_MODEL)) * (D_FF ** -0.5) * BETA,
    }


def reference(x_prompt, x_sample, w_in, b_in, ml_f_bias, ml_norm_g, na_rpb, df_lambda, df_norm_g,
              w_out, ln_g, ln_b, w_router, b_router, w_gate, w_up, w_down):
    y_prompt = _trunk(x_prompt, w_in, b_in, ml_f_bias, ml_norm_g, na_rpb, df_lambda, df_norm_g, w_out,
                      ln_g, ln_b, w_router, b_router, w_gate, w_up, w_down)
    y_sample = _trunk(x_sample, w_in, b_in, ml_f_bias, ml_norm_g, na_rpb, df_lambda, df_norm_g, w_out,
                      ln_g, ln_b, w_router, b_router, w_gate, w_up, w_down)
    return (y_prompt, y_sample)
```

```python
import functools
import math

import jax
import jax.numpy as jnp
from jax import lax
from jax.experimental import pallas as pl
from jax.experimental.pallas import tpu as pltpu

HEAD_DIM = 128
ML_HEADS = 8
NA_HEADS = 12
DF_HEADS = 12
ML_W = ML_HEADS * HEAD_DIM
NA_W = NA_HEADS * HEAD_DIM
DF_W = DF_HEADS * HEAD_DIM
DF_HALF = HEAD_DIM // 2
ROPE_DIM = DF_HALF // 4
ROPE_HALF = ROPE_DIM // 2
ROPE_THETA = 500000.0
GRID_W = 64
NA_WIN_R = 8
NA_WIN_C = 16
N_EXPERTS = 32
N_GROUPS = 8
EXPERTS_PER_GROUP = N_EXPERTS // N_GROUPS
TOP_K = 2
LN_EPS = 1e-5
N_GATE_COLS = 4 * ML_HEADS

LANES = 128
VMEM_LIMIT = 56 * 1024 * 1024

PROJ_TM = 1024
PROJ_TN = 512
LN_TM = 256
ML_CHUNK = 256
NA_QROWS = 8
NA_KROWS = 2 * NA_WIN_R
NA_BQ = NA_QROWS * GRID_W
NA_BK = NA_KROWS * GRID_W
DF_BQ = 512
DF_BK = 512
ROPE_BT = 512
ROUTE_TM = 512
MOE_TM = 512
MOE_TF = 256
NEG_BIG = -1e30

_BF16 = jnp.bfloat16
_F32 = jnp.float32


def _cparams(*sem):
    return pltpu.CompilerParams(dimension_semantics=sem, vmem_limit_bytes=VMEM_LIMIT)


def _proj_kernel(*refs, has_bias, has_scale, res_alpha):
    x_ref, w_ref = refs[:2]
    pos = 2
    acc = jnp.dot(x_ref[...], w_ref[...], preferred_element_type=_F32)
    if has_bias:
        acc = acc + refs[pos][...]
        pos += 1
    if has_scale:
        acc = acc * refs[pos][...]
        pos += 1
    if res_alpha is not None:
        acc = acc + res_alpha * refs[pos][...]
        pos += 1
    o_ref = refs[pos]
    o_ref[...] = acc.astype(o_ref.dtype)


def _proj(x, w, b, *, col_off, ncols, out_dtype, scale=None, res=None, res_alpha=None,
          tm=PROJ_TM, tn=PROJ_TN, name="proj"):
    m, kdim = x.shape
    tm = min(tm, m)
    tn = min(tn, ncols)
    assert m % tm == 0 and ncols % tn == 0 and col_off % tn == 0
    joff = col_off // tn
    in_specs = [
        pl.BlockSpec((tm, kdim), lambda i, j: (i, 0)),
        pl.BlockSpec((kdim, tn), lambda i, j: (0, j + joff)),
    ]
    args = [x, w]
    if b is not None:
        in_specs.append(pl.BlockSpec((1, tn), lambda i, j: (0, j + joff)))
        args.append(b)
    if scale is not None:
        in_specs.append(pl.BlockSpec((1, tn), lambda i, j: (0, j + joff)))
        args.append(scale)
    if res is not None:
        in_specs.append(pl.BlockSpec((tm, tn), lambda i, j: (i, j)))
        args.append(res)
    return pl.pallas_call(
        functools.partial(_proj_kernel, has_bias=b is not None, has_scale=scale is not None,
                          res_alpha=res_alpha),
        grid=(m // tm, ncols // tn),
        in_specs=in_specs,
        out_specs=pl.BlockSpec((tm, tn), lambda i, j: (i, j)),
        out_shape=jax.ShapeDtypeStruct((m, ncols), out_dtype),
        compiler_params=_cparams("parallel", "arbitrary"),
        name=name,
    )(*args)


def _ln_rows(z, g, b):
    mu = jnp.mean(z, axis=-1, keepdims=True)
    zc = z - mu
    var = jnp.mean(zc * zc, axis=-1, keepdims=True)
    return zc * lax.rsqrt(var + LN_EPS) * g + b


def _ln_kernel(z_ref, g_ref, b_ref, o_ref, ob_ref):
    y = _ln_rows(z_ref[...], g_ref[...], b_ref[...])
    o_ref[...] = y
    ob_ref[...] = y.astype(_BF16)


def _combine_ln_kernel(x_ref, y0_ref, y1_ref, gt_ref, g_ref, b_ref, o_ref, ob_ref, *, alpha):
    gt = gt_ref[...]
    moe = y0_ref[...] * gt[:, 0:1] + y1_ref[...] * gt[:, 1:2]
    y = _ln_rows(alpha * x_ref[...] + moe, g_ref[...], b_ref[...])
    o_ref[...] = y
    ob_ref[...] = y.astype(_BF16)


def _layer_norm(z, g, b):
    n, d = z.shape
    tm = min(LN_TM, n)
    row = pl.BlockSpec((tm, d), lambda i: (i, 0))
    vec = pl.BlockSpec((1, d), lambda i: (0, 0))
    return pl.pallas_call(
        _ln_kernel, grid=(n // tm,), in_specs=[row, vec, vec], out_specs=[row, row],
        out_shape=[jax.ShapeDtypeStruct((n, d), _F32), jax.ShapeDtypeStruct((n, d), _BF16)],
        compiler_params=_cparams("parallel"), name="layer_norm",
    )(z, g.reshape(1, d), b.reshape(1, d))


def _combine_layer_norm(x, y0, y1, gates_t, g, b, alpha):
    n, d = x.shape
    tm = min(LN_TM, n)
    row = pl.BlockSpec((tm, d), lambda i: (i, 0))
    vec = pl.BlockSpec((1, d), lambda i: (0, 0))
    return pl.pallas_call(
        functools.partial(_combine_ln_kernel, alpha=alpha),
        grid=(n // tm,),
        in_specs=[row, row, row, pl.BlockSpec((tm, TOP_K), lambda i: (i, 0)), vec, vec],
        out_specs=[row, row],
        out_shape=[jax.ShapeDtypeStruct((n, d), _F32), jax.ShapeDtypeStruct((n, d), _BF16)],
        compiler_params=_cparams("parallel"), name="combine_layer_norm",
    )(x, y0, y1, gates_t, g.reshape(1, d), b.reshape(1, d))


def _mlstm_kernel(fb_ref, q_ref, k_ref, v_ref, ig_ref, fg_ref, *rest, reverse):
    if reverse:
        hf_ref, og_ref, ng_ref, out_ref, c_ref, n_ref, m_ref = rest
    else:
        out_ref, c_ref, n_ref, m_ref = rest
    L, d = q_ref.shape
    head = pl.program_id(1)

    @pl.when(pl.program_id(2) == 0)
    def _():
        c_ref[...] = jnp.zeros_like(c_ref)
        n_ref[...] = jnp.zeros_like(n_ref)
        m_ref[...] = jnp.zeros_like(m_ref)

    ii = ig_ref[...]
    lf = jax.nn.log_sigmoid(fg_ref[...] + fb_ref[1 if reverse else 0, head])
    rows = lax.broadcasted_iota(jnp.int32, (L, L), 0)
    cols = lax.broadcasted_iota(jnp.int32, (L, L), 1)
    eye = rows == cols
    if reverse:
        visible = cols >= rows
        csum = (rows >= cols).astype(_F32)
    else:
        visible = cols <= rows
        csum = (rows <= cols).astype(_F32)
    b_r = jnp.dot(jnp.broadcast_to(lf, (8, L)), csum, precision=lax.Precision.HIGHEST,
                  preferred_element_type=_F32)[0:1]
    g = jnp.sum(lf, axis=-1, keepdims=True)

    def to_col(r):
        return jnp.sum(jnp.where(eye, r, 0.0), axis=1, keepdims=True)

    b_c = to_col(b_r)
    a_r = g - b_r + ii
    m_loc = jnp.max(a_r, axis=-1, keepdims=True)
    w_c = to_col(jnp.exp(a_r - m_loc))

    q = q_ref[...]
    k = k_ref[...] * (d ** -0.5)
    qb = q.astype(_BF16)
    kb = k.astype(_BF16)
    vb = v_ref[...].astype(_BF16)
    m_prev = m_ref[...]

    dmat = jnp.where(visible, b_c - (b_r - ii), -jnp.inf)
    m_inter = b_c + m_prev
    m_row = jnp.maximum(m_inter, jnp.max(dmat, axis=1, keepdims=True))
    w_inter = jnp.exp(m_inter - m_row)
    qk = lax.dot_general(qb, kb, (((1,), (1,)), ((), ())), preferred_element_type=_F32)
    s = jnp.exp(dmat - m_row) * qk
    q_c = jnp.dot(qb, c_ref[...].astype(_BF16), preferred_element_type=_F32)
    q_n = jnp.sum(q * n_ref[...], axis=1, keepdims=True)
    num = w_inter * q_c + jnp.dot(s.astype(_BF16), vb, preferred_element_type=_F32)
    den = w_inter * q_n + jnp.sum(s, axis=1, keepdims=True)
    h = num / jnp.maximum(jnp.abs(den), jnp.exp(-m_row))

    kw = k * w_c
    c_loc = lax.dot_general(kw.astype(_BF16), vb, (((0,), (0,)), ((), ())), preferred_element_type=_F32)
    n_loc = jnp.sum(kw, axis=0, keepdims=True)
    gm = g + m_prev
    m_new = jnp.maximum(gm, m_loc)
    s_old = jnp.exp(gm - m_new)
    s_new = jnp.exp(m_loc - m_new)
    c_ref[...] = s_old * c_ref[...] + s_new * c_loc
    n_ref[...] = s_old * n_ref[...] + s_new * n_loc
    m_ref[...] = m_new

    if reverse:
        h = h + hf_ref[...]
        mu = jnp.mean(h, axis=-1, keepdims=True)
        hc = h - mu
        var = jnp.mean(hc * hc, axis=-1, keepdims=True)
        h = hc * lax.rsqrt(var + LN_EPS) * ng_ref[...]
        out_ref[...] = (h * jax.nn.sigmoid(og_ref[...])).astype(out_ref.dtype)
    else:
        out_ref[...] = h


def _mlstm_mixer(p_ml, gates, f_bias, norm_g, *, batch, seq, b_off):
    L = min(ML_CHUNK, seq)
    nc = seq // L
    p3 = p_ml.reshape(-1, seq, 4 * ML_W)
    grid = (batch, ML_HEADS, nc)

    def run(reverse, extra_args, extra_specs, out_dtype):
        cidx = (lambda c: nc - 1 - c) if reverse else (lambda c: c)
        tok = lambda part: pl.BlockSpec((None, L, HEAD_DIM),
                                        lambda b, h, c: (b + b_off, cidx(c), part * ML_HEADS + h))
        gate = lambda which: pl.BlockSpec((None, None, None, None, 1, L),
                                          lambda b, h, c: (which, b, h, cidx(c), 0, 0))
        gi, gf = (2, 3) if reverse else (0, 1)
        out_spec = pl.BlockSpec((None, L, HEAD_DIM), lambda b, h, c: (b, cidx(c), h))
        in_specs = [pl.BlockSpec(memory_space=pltpu.SMEM), tok(0), tok(1), tok(2), gate(gi), gate(gf)]
        return pl.pallas_call(
            functools.partial(_mlstm_kernel, reverse=reverse),
            grid=grid,
            in_specs=in_specs + extra_specs(tok, out_spec),
            out_specs=out_spec,
            out_shape=jax.ShapeDtypeStruct((batch, seq, ML_W), out_dtype),
            scratch_shapes=[pltpu.VMEM((HEAD_DIM, HEAD_DIM), _F32), pltpu.VMEM((1, HEAD_DIM), _F32),
                            pltpu.VMEM((1, 1), _F32)],
            compiler_params=_cparams("parallel", "parallel", "arbitrary"),
            name="mlstm_reverse" if reverse else "mlstm_forward",
        )(f_bias, p3, p3, p3, gates, gates, *extra_args)

    h_f = run(False, (), lambda tok, out_spec: [], _F32)
    ng = norm_g.reshape(1, ML_W)
    y = run(True, (h_f, p3, ng),
            lambda tok, out_spec: [out_spec, tok(3), pl.BlockSpec((1, HEAD_DIM), lambda b, h, c: (0, h))],
            _BF16)
    return y.reshape(batch * seq, ML_W)


def _natten_tables(rpb):
    qr = jnp.arange(NA_QROWS)[:, None, None, None]
    qc = jnp.arange(GRID_W)[None, :, None, None]
    kr = jnp.arange(NA_KROWS)[None, None, :, None]
    kc = jnp.arange(GRID_W)[None, None, None, :]
    half = NA_WIN_R // 2
    rel = (kr - qr, kr - qr - half, kr - qr - NA_QROWS)
    first = (jnp.maximum(qr - half, 0), qr + 0 * kr, jnp.minimum(qr + half, NA_QROWS))
    cs = jnp.clip(qc - NA_WIN_C // 2, 0, GRID_W - NA_WIN_C)
    col_ok = (kc >= cs) & (kc < cs + NA_WIN_C)
    dc_idx = jnp.clip(kc - qc, -(NA_WIN_C - 1), NA_WIN_C - 1) + NA_WIN_C - 1
    tabs = []
    for dr, fr in zip(rel, first):
        row_ok = (kr >= fr) & (kr < fr + NA_WIN_R)
        ok = jnp.broadcast_to(row_ok & col_ok, (NA_QROWS, GRID_W, NA_KROWS, GRID_W))
        dr_idx = jnp.clip(dr + NA_WIN_R - 1, 0, 2 * NA_WIN_R - 2)
        dr_b = jnp.broadcast_to(dr_idx, ok.shape)
        dc_b = jnp.broadcast_to(dc_idx, ok.shape)
        bias = rpb.astype(_F32)[:, dr_b, dc_b]
        tabs.append(jnp.where(ok[None], bias, NEG_BIG).reshape(NA_HEADS, NA_BQ, NA_BK))
    return jnp.stack(tabs)


def _natten_kernel(q_ref, k_ref, v_ref, tab_ref, o_ref, *, nblk):
    i = pl.program_id(2)
    krow0 = jnp.clip(NA_QROWS * i - NA_WIN_R // 2, 0, nblk * NA_QROWS - NA_KROWS)
    start = pl.multiple_of(krow0 * GRID_W, NA_WIN_R // 2 * GRID_W)
    kk = k_ref[pl.ds(start, NA_BK), :]
    vv = v_ref[pl.ds(start, NA_BK), :]
    s = lax.dot_general(q_ref[...], kk, (((1,), (1,)), ((), ())), preferred_element_type=_F32) + tab_ref[...]
    m = jnp.max(s, axis=-1, keepdims=True)
    p = jnp.exp(s - m)
    l = jnp.sum(p, axis=-1, keepdims=True)
    o = jnp.dot(p.astype(_BF16), vv, preferred_element_type=_F32)
    o_ref[...] = (o / l).astype(o_ref.dtype)


def _natten_mixer(p_na, tables, *, batch, seq, b_off):
    nblk = seq // NA_BQ
    assert nblk >= 2 and seq % NA_BQ == 0
    p3 = p_na.reshape(-1, seq, 3 * NA_W)

    def variant(i):
        return jnp.where(i == 0, 0, jnp.where(i == nblk - 1, 2, 1))

    full = lambda part: pl.BlockSpec((None, seq, HEAD_DIM), lambda b, h, i: (b + b_off, 0, part * NA_HEADS + h))
    y = pl.pallas_call(
        functools.partial(_natten_kernel, nblk=nblk),
        grid=(batch, NA_HEADS, nblk),
        in_specs=[pl.BlockSpec((None, NA_BQ, HEAD_DIM), lambda b, h, i: (b + b_off, i, h)),
                  full(1), full(2),
                  pl.BlockSpec((None, None, NA_BQ, NA_BK), lambda b, h, i: (variant(i), h, 0, 0))],
        out_specs=pl.BlockSpec((None, NA_BQ, HEAD_DIM), lambda b, h, i: (b, i, h)),
        out_shape=jax.ShapeDtypeStruct((batch, seq, NA_W), _BF16),
        compiler_params=_cparams("parallel", "parallel", "arbitrary"),
        name="natten",
    )(p3, p3, p3, tables)
    return y.reshape(batch * seq, NA_W)


def _rope_tables(seq):
    pos = jnp.arange(seq, dtype=_F32)
    inv_freq = jnp.float32(ROPE_THETA) ** (-jnp.arange(0, ROPE_DIM, 2, dtype=_F32) / ROPE_DIM)
    ang = pos[:, None] * inv_freq[None, :]
    cos, sin = jnp.cos(ang), jnp.sin(ang)
    one = jnp.ones((seq, DF_HALF - ROPE_DIM), _F32)
    zero = jnp.zeros((seq, ROPE_HALF), _F32)
    zrest = jnp.zeros((seq, DF_HALF - ROPE_DIM), _F32)
    cos_h = jnp.concatenate([cos, cos, one], axis=1)
    up_h = jnp.concatenate([-sin, zero, zrest], axis=1)
    dn_h = jnp.concatenate([zero, sin, zrest], axis=1)
    two = lambda t: jnp.concatenate([t, t], axis=1)
    return two(cos_h), two(up_h), two(dn_h)


def _rope_kernel(t_ref, cos_ref, up_ref, dn_ref, *out_refs, split):
    cos, up, dn = cos_ref[...], up_ref[...], dn_ref[...]
    lane = lax.broadcasted_iota(jnp.int32, cos.shape, 1)
    for hd in range(t_ref.shape[1] // HEAD_DIM):
        sl = slice(hd * HEAD_DIM, (hd + 1) * HEAD_DIM)
        t = t_ref[:, sl]
        r = t * cos + pltpu.roll(t, HEAD_DIM - ROPE_HALF, 1) * up + pltpu.roll(t, ROPE_HALF, 1) * dn
        if split:
            out_refs[0][:, sl] = jnp.where(lane < DF_HALF, r, 0.0).astype(_BF16)
            out_refs[1][:, sl] = jnp.where(lane >= DF_HALF, r, 0.0).astype(_BF16)
        else:
            out_refs[0][:, sl] = r.astype(_BF16)


def _rope(p_dqk, tables, *, batch, seq, b_off, part, split):
    bt = min(ROPE_BT, seq)
    p3 = p_dqk.reshape(-1, seq, 2 * DF_W)
    tab = pl.BlockSpec((bt, HEAD_DIM), lambda b, i: (i, 0))
    out = pl.BlockSpec((None, bt, DF_W), lambda b, i: (b, i, 0))
    n_out = 2 if split else 1
    res = pl.pallas_call(
        functools.partial(_rope_kernel, split=split),
        grid=(batch, seq // bt),
        in_specs=[pl.BlockSpec((None, bt, DF_W), lambda b, i: (b + b_off, i, part)), tab, tab, tab],
        out_specs=[out] * n_out,
        out_shape=[jax.ShapeDtypeStruct((batch, seq, DF_W), _BF16)] * n_out,
        compiler_params=_cparams("parallel", "parallel"),
        name="rope_k" if split else "rope_q",
    )(p3, *tables)
    return res


def _diff_kernel(q_ref, ka_ref, kb_ref, v_ref, lp_ref, ng_ref, o_ref, m_ref, l_ref, acc_ref, *, lambda_init, bk):
    seq = ka_ref.shape[0]
    q = q_ref[...]
    m_ref[...] = jnp.full_like(m_ref, -jnp.inf)
    l_ref[...] = jnp.zeros_like(l_ref)
    acc_ref[...] = jnp.zeros_like(acc_ref)

    def body(j, carry):
        start = pl.multiple_of(j * bk, bk)
        vv = v_ref[pl.ds(start, bk), :]
        for mp, k_ref in enumerate((ka_ref, kb_ref)):
            kk = k_ref[pl.ds(start, bk), :]
            s = lax.dot_general(q, kk, (((1,), (1,)), ((), ())), preferred_element_type=_F32)
            m_old = m_ref[mp]
            m_new = jnp.maximum(m_old, jnp.max(s, axis=-1, keepdims=True))
            alpha = jnp.exp(m_old - m_new)
            p = jnp.exp(s - m_new)
            l_ref[mp] = alpha * l_ref[mp] + jnp.sum(p, axis=-1, keepdims=True)
            acc_ref[mp] = alpha * acc_ref[mp] + jnp.dot(p.astype(_BF16), vv, preferred_element_type=_F32)
            m_ref[mp] = m_new
        return carry

    lax.fori_loop(0, seq // bk, body, 0)

    lp = lp_ref[...]
    dots = jnp.sum(lp[0:1] * lp[1:2], axis=-1, keepdims=True), jnp.sum(lp[2:3] * lp[3:4], axis=-1, keepdims=True)
    lam = jnp.exp(dots[0]) - jnp.exp(dots[1]) + lambda_init
    o = acc_ref[0] / l_ref[0] - lam * (acc_ref[1] / l_ref[1])
    o = o * lax.rsqrt(jnp.mean(o * o, axis=-1, keepdims=True) + LN_EPS) * ng_ref[...] * (1.0 - lambda_init)
    o_ref[...] = o.astype(o_ref.dtype)


def _diff_mixer(q_rot, k_a, k_b, p_dv, lam_p, norm_g, lambda_init, *, batch, seq, b_off):
    bq = min(DF_BQ, seq)
    bk = min(DF_BK, seq)
    v3 = p_dv.reshape(-1, seq, DF_W)
    full = lambda off: pl.BlockSpec((None, seq, HEAD_DIM), lambda b, h, i: (b + off, 0, h))
    blk = pl.BlockSpec((None, bq, HEAD_DIM), lambda b, h, i: (b, i, h))
    y = pl.pallas_call(
        functools.partial(_diff_kernel, lambda_init=lambda_init, bk=bk),
        grid=(batch, DF_HEADS, seq // bq),
        in_specs=[blk, full(0), full(0), full(b_off),
                  pl.BlockSpec((4, DF_HALF), lambda b, h, i: (0, 0)),
                  pl.BlockSpec((1, HEAD_DIM), lambda b, h, i: (0, 0))],
        out_specs=blk,
        out_shape=jax.ShapeDtypeStruct((batch, seq, DF_W), _BF16),
        scratch_shapes=[pltpu.VMEM((2, bq, 1), _F32), pltpu.VMEM((2, bq, 1), _F32),
                        pltpu.VMEM((2, bq, HEAD_DIM), _F32)],
        compiler_params=_cparams("parallel", "parallel", "arbitrary"),
        name="diff_attention",
    )(q_rot, k_a, k_b, v3, lam_p.astype(_F32), norm_g.reshape(1, HEAD_DIM).astype(_F32))
    return y.reshape(batch * seq, DF_W)


def _router_kernel(x_ref, w_ref, b_ref, idx_ref, gate_ref):
    logits = lax.dot_general(w_ref[...], x_ref[...], (((1,), (1,)), ((), ())),
                             preferred_element_type=_F32) + b_ref[...]
    mx = jnp.max(logits, axis=0, keepdims=True)
    ex = jnp.exp(logits - mx)
    probs = ex / jnp.sum(ex, axis=0, keepdims=True)
    e = [probs[j * N_GROUPS:(j + 1) * N_GROUPS] for j in range(EXPERTS_PER_GROUP)]
    hi01, lo01 = jnp.maximum(e[0], e[1]), jnp.minimum(e[0], e[1])
    hi23, lo23 = jnp.maximum(e[2], e[3]), jnp.minimum(e[2], e[3])
    gscore = jnp.maximum(hi01, hi23) + jnp.maximum(jnp.minimum(hi01, hi23), jnp.maximum(lo01, lo23))
    gid = lax.broadcasted_iota(jnp.int32, gscore.shape, 0).astype(_F32)
    best = jnp.min(jnp.where(gscore == jnp.max(gscore, axis=0, keepdims=True), gid, float(N_GROUPS)),
                   axis=0, keepdims=True)
    within = [jnp.sum(jnp.where(gid == best, ej, 0.0), axis=0, keepdims=True) for ej in e]

    def first_max(vals):
        top = functools.reduce(jnp.maximum, vals)
        pos = jnp.full(top.shape, float(EXPERTS_PER_GROUP), _F32)
        for j in reversed(range(EXPERTS_PER_GROUP)):
            pos = jnp.where(vals[j] == top, float(j), pos)
        return top, pos

    v1, i1 = first_max(within)
    v2, i2 = first_max([jnp.where(i1 == float(j), -1.0, within[j]) for j in range(EXPERTS_PER_GROUP)])
    base = best * EXPERTS_PER_GROUP
    idx_ref[0:1, :] = (base + i1).astype(jnp.int32)
    idx_ref[1:2, :] = (base + i2).astype(jnp.int32)
    tot = v1 + v2
    gate_ref[0:1, :] = v1 / tot
    gate_ref[1:2, :] = v2 / tot


def _route(xb, w_router, b_router):
    n, d = xb.shape
    tm = min(ROUTE_TM, n)
    perm = (jnp.arange(N_EXPERTS) % N_GROUPS) * EXPERTS_PER_GROUP + jnp.arange(N_EXPERTS) // N_GROUPS
    w_t = w_router.T[perm].astype(_BF16)
    b_c = b_router.astype(_F32)[perm].reshape(N_EXPERTS, 1)
    return pl.pallas_call(
        _router_kernel, grid=(n // tm,),
        in_specs=[pl.BlockSpec((tm, d), lambda i: (i, 0)),
                  pl.BlockSpec((N_EXPERTS, d), lambda i: (0, 0)),
                  pl.BlockSpec((N_EXPERTS, 1), lambda i: (0, 0))],
        out_specs=[pl.BlockSpec((TOP_K, tm), lambda i: (0, i))] * 2,
        out_shape=[jax.ShapeDtypeStruct((TOP_K, n), jnp.int32), jax.ShapeDtypeStruct((TOP_K, n), _F32)],
        compiler_params=_cparams("parallel"), name="router",
    )(xb, w_t, b_c)


def _expert_kernel(be_ref, nv_ref, x_ref, wg_ref, wu_ref, wd_ref, o_ref):
    blk, f = pl.program_id(0), pl.program_id(1)

    @pl.when(f == 0)
    def _():
        o_ref[...] = jnp.zeros_like(o_ref)

    @pl.when(blk < nv_ref[0])
    def _():
        x = x_ref[...]
        hg = jnp.dot(x, wg_ref[...], preferred_element_type=_F32)
        hu = jnp.dot(x, wu_ref[...], preferred_element_type=_F32)
        h = (hg * jax.nn.sigmoid(hg) * hu).astype(_BF16)
        o_ref[...] += jnp.dot(h, wd_ref[...], preferred_element_type=_F32)


def _experts(x_rows, block_expert, n_live, wg, wu, wd, *, tm):
    p_rows, d = x_rows.shape
    nb = p_rows // tm
    ff = wg.shape[-1]
    tf = min(MOE_TF, ff)
    nf = ff // tf

    def fsel(b, f, nv):
        return jnp.where(b < nv[0], f, nf - 1)

    grid_spec = pltpu.PrefetchScalarGridSpec(
        num_scalar_prefetch=2,
        grid=(nb, nf),
        in_specs=[
            pl.BlockSpec((tm, d), lambda b, f, be, nv: (b, 0)),
            pl.BlockSpec((None, d, tf), lambda b, f, be, nv: (be[b], 0, fsel(b, f, nv))),
            pl.BlockSpec((None, d, tf), lambda b, f, be, nv: (be[b], 0, fsel(b, f, nv))),
            pl.BlockSpec((None, tf, d), lambda b, f, be, nv: (be[b], fsel(b, f, nv), 0)),
        ],
        out_specs=pl.BlockSpec((tm, d), lambda b, f, be, nv: (b, 0)),
    )
    return pl.pallas_call(
        _expert_kernel, grid_spec=grid_spec,
        out_shape=jax.ShapeDtypeStruct((p_rows, d), _F32),
        compiler_params=_cparams("arbitrary", "arbitrary"), name="experts",
    )(block_expert, n_live, x_rows, wg, wu, wd)


def _dispatch_plan(idx, n_tokens, tm):
    a_tot = n_tokens * TOP_K
    nb = (a_tot + N_EXPERTS * (tm - 1) + tm - 1) // tm
    flat_e = idx.reshape(-1)
    onehot = (flat_e[:, None] == jnp.arange(N_EXPERTS)[None, :]).astype(jnp.int32)
    before = jnp.cumsum(onehot, axis=0) - onehot
    rank = jnp.sum(before * onehot, axis=1)
    counts = jnp.sum(onehot, axis=0)
    padded = (counts + tm - 1) // tm * tm
    pend = jnp.cumsum(padded)
    poff = pend - padded
    dest = poff[flat_e] + rank
    tok = jnp.tile(jnp.arange(n_tokens, dtype=jnp.int32), TOP_K)
    row_tok = jnp.full((nb * tm,), n_tokens, jnp.int32).at[dest].set(tok)
    block_expert = jnp.minimum(jnp.searchsorted(pend, jnp.arange(nb) * tm, side='right'),
                               N_EXPERTS - 1).astype(jnp.int32)
    n_live = (pend[-1] // tm).astype(jnp.int32).reshape(1)
    return row_tok, dest.reshape(TOP_K, n_tokens), block_expert, n_live


def _moe_block(x, xb, w_router, b_router, wg, wu, wd, ln_g, ln_b, alpha):
    n, d = x.shape
    tm = min(MOE_TM, n)
    idx, gates = _route(xb, w_router, b_router)
    row_tok, pos, block_expert, n_live = _dispatch_plan(idx, n, tm)
    x_pad = jnp.concatenate([xb, jnp.zeros((1, d), xb.dtype)], axis=0)
    yb = _experts(x_pad[row_tok], block_expert, n_live, wg, wu, wd, tm=tm)
    return _combine_layer_norm(x, yb[pos[0]], yb[pos[1]], gates.T, ln_g, ln_b, alpha)


def _split_w_in(w_in_l, b_in_l):
    g0 = 4 * ML_W
    g1 = g0 + N_GATE_COLS
    w_main = jnp.concatenate([w_in_l[:, :g0], w_in_l[:, g1:]], axis=1).astype(_BF16)
    b_main = jnp.concatenate([b_in_l[:g0], b_in_l[g1:]]).astype(_F32).reshape(1, -1)
    pad = LANES - N_GATE_COLS
    w_gate = jnp.pad(w_in_l[:, g0:g1], ((0, 0), (0, pad))).astype(_BF16)
    b_gate = jnp.pad(b_in_l[g0:g1], (0, pad)).astype(_F32).reshape(1, -1)
    return w_main, b_main, w_gate, b_gate


def _trunk_layers(x, geoms, depth, w_in, b_in, ml_f_bias, ml_norm_g, na_rpb, df_lambda, df_norm_g, w_out,
                  ln_g, ln_b, w_router, b_router, w_gate, w_up, w_down):
    n, d = x.shape
    alpha = (2 * depth) ** 0.25
    xb = x.astype(_BF16)
    off_ml, off_na, off_dqk, off_dv = 0, 4 * ML_W, 4 * ML_W + 3 * NA_W, 4 * ML_W + 3 * NA_W + 2 * DF_W
    na_scale = jnp.ones((off_dv + DF_W,), _F32).at[off_na:off_na + NA_W].set(HEAD_DIM ** -0.5).reshape(1, -1)
    q_scale = DF_HALF ** -0.5
    rope_tabs = {}
    for _, seq in geoms:
        if seq not in rope_tabs:
            cos, up, dn = _rope_tables(seq)
            rope_tabs[seq] = ((cos * q_scale, up * q_scale, dn * q_scale), (cos, up, dn))

    for l in range(depth):
        lambda_init = 0.8 - 0.6 * math.exp(-0.3 * l)
        w_main, b_main, w_g, b_g = _split_w_in(w_in[l], b_in[l])
        p_ml = _proj(xb, w_main, b_main, col_off=off_ml, ncols=4 * ML_W, out_dtype=_F32, name="proj_mlstm")
        p_na = _proj(xb, w_main, b_main, col_off=off_na, ncols=3 * NA_W, out_dtype=_BF16, scale=na_scale,
                     name="proj_natten")
        p_dqk = _proj(xb, w_main, b_main, col_off=off_dqk, ncols=2 * DF_W, out_dtype=_F32, name="proj_diff_qk")
        p_dv = _proj(xb, w_main, b_main, col_off=off_dv, ncols=DF_W, out_dtype=_BF16, name="proj_diff_v")
        p_g = _proj(xb, w_g, b_g, col_off=0, ncols=LANES, out_dtype=_F32, name="proj_gates")
        tables = _natten_tables(na_rpb[l])

        mixed = []
        tok0 = 0
        for batch, seq in geoms:
            b_off = tok0 // seq
            assert b_off * seq == tok0
            L = min(ML_CHUNK, seq)
            gts = p_g[tok0:tok0 + batch * seq, :N_GATE_COLS].reshape(batch, seq, 4, ML_HEADS)
            gts = gts.transpose(2, 0, 3, 1).reshape(4, batch, ML_HEADS, seq // L, 1, L)
            y_ml = _mlstm_mixer(p_ml, gts, ml_f_bias[l].astype(_F32), ml_norm_g[l].astype(_F32),
                                batch=batch, seq=seq, b_off=b_off)
            y_na = _natten_mixer(p_na, tables, batch=batch, seq=seq, b_off=b_off)
            tq, tk = rope_tabs[seq]
            (q_rot,) = _rope(p_dqk, tq, batch=batch, seq=seq, b_off=b_off, part=0, split=False)
            k_a, k_b = _rope(p_dqk, tk, batch=batch, seq=seq, b_off=b_off, part=1, split=True)
            y_df = _diff_mixer(q_rot, k_a, k_b, p_dv, df_lambda[l], df_norm_g[l], lambda_init,
                               batch=batch, seq=seq, b_off=b_off)
            mixed.append(jnp.concatenate([y_ml, y_na, y_df], axis=1))
            tok0 += batch * seq
        y_mix = jnp.concatenate(mixed, axis=0) if len(mixed) > 1 else mixed[0]

        z = _proj(y_mix, w_out[l].astype(_BF16), None, col_off=0, ncols=d, out_dtype=_F32, res=x, res_alpha=alpha,
                  name="proj_out")
        x, xb = _layer_norm(z, ln_g[l, 0].astype(_F32), ln_b[l, 0].astype(_F32))
        x, xb = _moe_block(x, xb, w_router, b_router, w_gate[l].astype(_BF16), w_up[l].astype(_BF16),
                           w_down[l].astype(_BF16), ln_g[l, 1].astype(_F32), ln_b[l, 1].astype(_F32), alpha)
    return x


@jax.jit
def kernel(x_prompt, x_sample, w_in, b_in, ml_f_bias, ml_norm_g, na_rpb, df_lambda, df_norm_g, w_out, ln_g, ln_b,
           w_router, b_router, w_gate, w_up, w_down):
    d = x_prompt.shape[-1]
    geoms = [x_prompt.shape[:2], x_sample.shape[:2]]
    x = jnp.concatenate([x_prompt.reshape(-1, d), x_sample.reshape(-1, d)], axis=0)
    y = _trunk_layers(x, geoms, w_in.shape[0], w_in, b_in, ml_f_bias, ml_norm_g, na_rpb, df_lambda, df_norm_g,
                      w_out, ln_g, ln_b, w_router, b_router, w_gate, w_up, w_down)
    n0 = x_prompt.shape[0] * x_prompt.shape[1]
    return y[:n0].reshape(x_prompt.shape), y[n0:].reshape(x_sample.shape)
```

```python
import functools
import math

import jax
import jax.numpy as jnp
import numpy as np
from jax import lax
from jax.experimental import pallas as pl
from jax.experimental.pallas import tpu as pltpu

HEAD_DIM = 128
ML_HEADS = 8
NA_HEADS = 12
DF_HEADS = 12
ML_W = ML_HEADS * HEAD_DIM
NA_W = NA_HEADS * HEAD_DIM
DF_W = DF_HEADS * HEAD_DIM
DF_HALF = HEAD_DIM // 2
ROPE_DIM = DF_HALF // 4
ROPE_HALF = ROPE_DIM // 2
ROPE_THETA = 500000.0
GRID_W = 64
NA_WIN_R = 8
NA_WIN_C = 16
N_EXPERTS = 32
N_GROUPS = 8
EXPERTS_PER_GROUP = N_EXPERTS // N_GROUPS
TOP_K = 2
LN_EPS = 1e-5
N_GATE_COLS = 4 * ML_HEADS

LANES = 128
VMEM_LIMIT = 56 * 1024 * 1024

PROJ_TM = 1024
PROJ_TN = 512
LN_TM = 256
ML_CHUNK = 256
NA_QROWS = 8
NA_KROWS = 2 * NA_WIN_R
NA_BQ = NA_QROWS * GRID_W
NA_BK = NA_KROWS * GRID_W
DF_BQ = 512
DF_BK = 512
DF_VT_ROWS = HEAD_DIM + 16
ROPE_BT = 512
ROUTE_TM = 512
MOE_TM = 512
MOE_TF = 256
NEG_BIG = -1e30

_BF16 = jnp.bfloat16
_F32 = jnp.float32


def _cparams(*sem):
    return pltpu.CompilerParams(dimension_semantics=sem, vmem_limit_bytes=VMEM_LIMIT)


def _proj_kernel(*refs, has_bias, has_scale, res_alpha):
    x_ref, w_ref = refs[:2]
    pos = 2
    acc = jnp.dot(x_ref[...], w_ref[...], preferred_element_type=_F32)
    if has_bias:
        acc = acc + refs[pos][...]
        pos += 1
    if has_scale:
        acc = acc * refs[pos][...]
        pos += 1
    if res_alpha is not None:
        acc = acc + res_alpha * refs[pos][...]
        pos += 1
    o_ref = refs[pos]
    o_ref[...] = acc.astype(o_ref.dtype)


def _proj(x, w, b, *, col_off, ncols, out_dtype, scale=None, res=None, res_alpha=None,
          tm=PROJ_TM, tn=PROJ_TN, name="proj"):
    m, kdim = x.shape
    tm = min(tm, m)
    tn = min(tn, ncols)
    assert m % tm == 0 and ncols % tn == 0 and col_off % tn == 0
    joff = col_off // tn
    in_specs = [
        pl.BlockSpec((tm, kdim), lambda i, j: (i, 0)),
        pl.BlockSpec((kdim, tn), lambda i, j: (0, j + joff)),
    ]
    args = [x, w]
    if b is not None:
        in_specs.append(pl.BlockSpec((1, tn), lambda i, j: (0, j + joff)))
        args.append(b)
    if scale is not None:
        in_specs.append(pl.BlockSpec((1, tn), lambda i, j: (0, j + joff)))
        args.append(scale)
    if res is not None:
        in_specs.append(pl.BlockSpec((tm, tn), lambda i, j: (i, j)))
        args.append(res)
    return pl.pallas_call(
        functools.partial(_proj_kernel, has_bias=b is not None, has_scale=scale is not None,
                          res_alpha=res_alpha),
        grid=(m // tm, ncols // tn),
        in_specs=in_specs,
        out_specs=pl.BlockSpec((tm, tn), lambda i, j: (i, j)),
        out_shape=jax.ShapeDtypeStruct((m, ncols), out_dtype),
        compiler_params=_cparams("parallel", "arbitrary"),
        name=name,
    )(*args)


def _ln_rows(z, g, b):
    mu = jnp.mean(z, axis=-1, keepdims=True)
    zc = z - mu
    var = jnp.mean(zc * zc, axis=-1, keepdims=True)
    return zc * lax.rsqrt(var + LN_EPS) * g + b


def _ln_kernel(z_ref, g_ref, b_ref, o_ref, ob_ref):
    y = _ln_rows(z_ref[...], g_ref[...], b_ref[...])
    o_ref[...] = y
    ob_ref[...] = y.astype(_BF16)


def _combine_ln_kernel(x_ref, y0_ref, y1_ref, gt_ref, g_ref, b_ref, o_ref, ob_ref, *, alpha):
    gt = gt_ref[...]
    moe = y0_ref[...] * gt[:, 0:1] + y1_ref[...] * gt[:, 1:2]
    y = _ln_rows(alpha * x_ref[...] + moe, g_ref[...], b_ref[...])
    o_ref[...] = y
    ob_ref[...] = y.astype(_BF16)


def _layer_norm(z, g, b):
    n, d = z.shape
    tm = min(LN_TM, n)
    row = pl.BlockSpec((tm, d), lambda i: (i, 0))
    vec = pl.BlockSpec((1, d), lambda i: (0, 0))
    return pl.pallas_call(
        _ln_kernel, grid=(n // tm,), in_specs=[row, vec, vec], out_specs=[row, row],
        out_shape=[jax.ShapeDtypeStruct((n, d), _F32), jax.ShapeDtypeStruct((n, d), _BF16)],
        compiler_params=_cparams("parallel"), name="layer_norm",
    )(z, g.reshape(1, d), b.reshape(1, d))


def _combine_layer_norm(x, y0, y1, gates_t, g, b, alpha):
    n, d = x.shape
    tm = min(LN_TM, n)
    row = pl.BlockSpec((tm, d), lambda i: (i, 0))
    vec = pl.BlockSpec((1, d), lambda i: (0, 0))
    return pl.pallas_call(
        functools.partial(_combine_ln_kernel, alpha=alpha),
        grid=(n // tm,),
        in_specs=[row, row, row, pl.BlockSpec((tm, TOP_K), lambda i: (i, 0)), vec, vec],
        out_specs=[row, row],
        out_shape=[jax.ShapeDtypeStruct((n, d), _F32), jax.ShapeDtypeStruct((n, d), _BF16)],
        compiler_params=_cparams("parallel"), name="combine_layer_norm",
    )(x, y0, y1, gates_t, g.reshape(1, d), b.reshape(1, d))


def _mlstm_kernel(fb_ref, q_ref, k_ref, v_ref, ig_ref, fg_ref, *rest, reverse):
    if reverse:
        hf_ref, og_ref, ng_ref, out_ref, c_ref, n_ref, m_ref = rest
    else:
        out_ref, c_ref, n_ref, m_ref = rest
    L, d = q_ref.shape
    head = pl.program_id(1)

    @pl.when(pl.program_id(2) == 0)
    def _():
        c_ref[...] = jnp.zeros_like(c_ref)
        n_ref[...] = jnp.zeros_like(n_ref)
        m_ref[...] = jnp.zeros_like(m_ref)

    ii = ig_ref[...]
    lf = jax.nn.log_sigmoid(fg_ref[...] + fb_ref[1 if reverse else 0, head])
    rows = lax.broadcasted_iota(jnp.int32, (L, L), 0)
    cols = lax.broadcasted_iota(jnp.int32, (L, L), 1)
    eye = rows == cols
    if reverse:
        visible = cols >= rows
        csum = (rows >= cols).astype(_F32)
    else:
        visible = cols <= rows
        csum = (rows <= cols).astype(_F32)
    b_r = jnp.dot(jnp.broadcast_to(lf, (8, L)), csum, precision=lax.Precision.HIGHEST,
                  preferred_element_type=_F32)[0:1]
    g = jnp.sum(lf, axis=-1, keepdims=True)

    def to_col(r):
        return jnp.sum(jnp.where(eye, r, 0.0), axis=1, keepdims=True)

    b_c = to_col(b_r)
    a_r = g - b_r + ii
    m_loc = jnp.max(a_r, axis=-1, keepdims=True)
    w_c = to_col(jnp.exp(a_r - m_loc))

    q = q_ref[...]
    k = k_ref[...] * (d ** -0.5)
    qb = q.astype(_BF16)
    kb = k.astype(_BF16)
    vb = v_ref[...].astype(_BF16)
    m_prev = m_ref[...]

    dmat = jnp.where(visible, b_c - (b_r - ii), -jnp.inf)
    m_inter = b_c + m_prev
    m_row = jnp.maximum(m_inter, jnp.max(dmat, axis=1, keepdims=True))
    w_inter = jnp.exp(m_inter - m_row)
    qk = lax.dot_general(qb, kb, (((1,), (1,)), ((), ())), preferred_element_type=_F32)
    s = jnp.exp(dmat - m_row) * qk
    q_c = jnp.dot(qb, c_ref[...].astype(_BF16), preferred_element_type=_F32)
    q_n = jnp.sum(q * n_ref[...], axis=1, keepdims=True)
    num = w_inter * q_c + jnp.dot(s.astype(_BF16), vb, preferred_element_type=_F32)
    den = w_inter * q_n + jnp.sum(s, axis=1, keepdims=True)
    h = num / jnp.maximum(jnp.abs(den), jnp.exp(-m_row))

    kw = k * w_c
    c_loc = lax.dot_general(kw.astype(_BF16), vb, (((0,), (0,)), ((), ())), preferred_element_type=_F32)
    n_loc = jnp.sum(kw, axis=0, keepdims=True)
    gm = g + m_prev
    m_new = jnp.maximum(gm, m_loc)
    s_old = jnp.exp(gm - m_new)
    s_new = jnp.exp(m_loc - m_new)
    c_ref[...] = s_old * c_ref[...] + s_new * c_loc
    n_ref[...] = s_old * n_ref[...] + s_new * n_loc
    m_ref[...] = m_new

    if reverse:
        h = h + hf_ref[...]
        mu = jnp.mean(h, axis=-1, keepdims=True)
        hc = h - mu
        var = jnp.mean(hc * hc, axis=-1, keepdims=True)
        h = hc * lax.rsqrt(var + LN_EPS) * ng_ref[...]
        out_ref[...] = (h * jax.nn.sigmoid(og_ref[...])).astype(out_ref.dtype)
    else:
        out_ref[...] = h


def _mlstm_mixer(p_ml, gates, f_bias, norm_g, *, batch, seq, b_off):
    L = min(ML_CHUNK, seq)
    nc = seq // L
    p3 = p_ml.reshape(-1, seq, 4 * ML_W)
    grid = (batch, ML_HEADS, nc)

    def run(reverse, extra_args, extra_specs, out_dtype):
        cidx = (lambda c: nc - 1 - c) if reverse else (lambda c: c)
        tok = lambda part: pl.BlockSpec((None, L, HEAD_DIM),
                                        lambda b, h, c: (b + b_off, cidx(c), part * ML_HEADS + h))
        gate = lambda which: pl.BlockSpec((None, None, None, None, 1, L),
                                          lambda b, h, c: (which, b, h, cidx(c), 0, 0))
        gi, gf = (2, 3) if reverse else (0, 1)
        out_spec = pl.BlockSpec((None, L, HEAD_DIM), lambda b, h, c: (b, cidx(c), h))
        in_specs = [pl.BlockSpec(memory_space=pltpu.SMEM), tok(0), tok(1), tok(2), gate(gi), gate(gf)]
        return pl.pallas_call(
            functools.partial(_mlstm_kernel, reverse=reverse),
            grid=grid,
            in_specs=in_specs + extra_specs(tok, out_spec),
            out_specs=out_spec,
            out_shape=jax.ShapeDtypeStruct((batch, seq, ML_W), out_dtype),
            scratch_shapes=[pltpu.VMEM((HEAD_DIM, HEAD_DIM), _F32), pltpu.VMEM((1, HEAD_DIM), _F32),
                            pltpu.VMEM((1, 1), _F32)],
            compiler_params=_cparams("parallel", "parallel", "arbitrary"),
            name="mlstm_reverse" if reverse else "mlstm_forward",
        )(f_bias, p3, p3, p3, gates, gates, *extra_args)

    h_f = run(False, (), lambda tok, out_spec: [], _F32)
    ng = norm_g.reshape(1, ML_W)
    y = run(True, (h_f, p3, ng),
            lambda tok, out_spec: [out_spec, tok(3), pl.BlockSpec((1, HEAD_DIM), lambda b, h, c: (0, h))],
            _BF16)
    return y.reshape(batch * seq, ML_W)


def _natten_tables(rpb):
    qr = np.arange(NA_QROWS)[:, None]
    kr = np.arange(NA_KROWS)[None, :]
    qc = np.arange(GRID_W)[:, None]
    kc = np.arange(GRID_W)[None, :]
    half = NA_WIN_R // 2
    cs = np.clip(qc - NA_WIN_C // 2, 0, GRID_W - NA_WIN_C)
    col_ok = (kc >= cs) & (kc < cs + NA_WIN_C)
    dc_idx = np.clip(kc - qc, -(NA_WIN_C - 1), NA_WIN_C - 1) + NA_WIN_C - 1
    col_bias = jnp.take(rpb.astype(_F32), jnp.asarray(dc_idx.reshape(-1)), axis=2)
    col_bias = col_bias.reshape(NA_HEADS, 2 * NA_WIN_R - 1, GRID_W, GRID_W)
    geoms = ((0, np.maximum(qr - half, 0)), (half, qr), (NA_QROWS, np.minimum(qr + half, NA_QROWS)))
    tabs = []
    for shift, first in geoms:
        dr_idx = np.clip(kr - qr - shift + NA_WIN_R - 1, 0, 2 * NA_WIN_R - 2)
        row_ok = (kr >= first) & (kr < first + NA_WIN_R)
        bias = jnp.take(col_bias, jnp.asarray(dr_idx.reshape(-1)), axis=1)
        bias = bias.reshape(NA_HEADS, NA_QROWS, NA_KROWS, GRID_W, GRID_W).transpose(0, 1, 3, 2, 4)
        ok = row_ok[:, None, :, None] & col_ok[None, :, None, :]
        tabs.append(jnp.where(jnp.asarray(ok)[None], bias, NEG_BIG).reshape(NA_HEADS, NA_BQ, NA_BK))
    return jnp.stack(tabs)


def _natten_kernel(q_ref, k_ref, v_ref, tab_ref, o_ref, *, nblk):
    i = pl.program_id(2)
    krow0 = jnp.clip(NA_QROWS * i - NA_WIN_R // 2, 0, nblk * NA_QROWS - NA_KROWS)
    start = pl.multiple_of(krow0 * GRID_W, NA_WIN_R // 2 * GRID_W)
    kk = k_ref[pl.ds(start, NA_BK), :]
    vv = v_ref[pl.ds(start, NA_BK), :]
    s = lax.dot_general(q_ref[...], kk, (((1,), (1,)), ((), ())), preferred_element_type=_F32) + tab_ref[...]
    m = jnp.max(s, axis=-1, keepdims=True)
    p = jnp.exp(s - m)
    l = jnp.sum(p, axis=-1, keepdims=True)
    o = jnp.dot(p.astype(_BF16), vv, preferred_element_type=_F32)
    o_ref[...] = (o / l).astype(o_ref.dtype)


def _natten_mixer(p_na, tables, *, batch, seq, b_off):
    nblk = seq // NA_BQ
    assert nblk >= 2 and seq % NA_BQ == 0
    p3 = p_na.reshape(-1, seq, 3 * NA_W)

    def variant(i):
        return jnp.where(i == 0, 0, jnp.where(i == nblk - 1, 2, 1))

    full = lambda part: pl.BlockSpec((None, seq, HEAD_DIM), lambda b, h, i: (b + b_off, 0, part * NA_HEADS + h))
    y = pl.pallas_call(
        functools.partial(_natten_kernel, nblk=nblk),
        grid=(batch, NA_HEADS, nblk),
        in_specs=[pl.BlockSpec((None, NA_BQ, HEAD_DIM), lambda b, h, i: (b + b_off, i, h)),
                  full(1), full(2),
                  pl.BlockSpec((None, None, NA_BQ, NA_BK), lambda b, h, i: (variant(i), h, 0, 0))],
        out_specs=pl.BlockSpec((None, NA_BQ, HEAD_DIM), lambda b, h, i: (b, i, h)),
        out_shape=jax.ShapeDtypeStruct((batch, seq, NA_W), _BF16),
        compiler_params=_cparams("parallel", "parallel", "arbitrary"),
        name="natten",
    )(p3, p3, p3, tables)
    return y.reshape(batch * seq, NA_W)


def _rope_tables(seq):
    pos = jnp.arange(seq, dtype=_F32)
    inv_freq = jnp.float32(ROPE_THETA) ** (-jnp.arange(0, ROPE_DIM, 2, dtype=_F32) / ROPE_DIM)
    ang = pos[:, None] * inv_freq[None, :]
    cos, sin = jnp.cos(ang), jnp.sin(ang)
    one = jnp.ones((seq, DF_HALF - ROPE_DIM), _F32)
    zero = jnp.zeros((seq, ROPE_HALF), _F32)
    zrest = jnp.zeros((seq, DF_HALF - ROPE_DIM), _F32)
    cos_h = jnp.concatenate([cos, cos, one], axis=1)
    up_h = jnp.concatenate([-sin, zero, zrest], axis=1)
    dn_h = jnp.concatenate([zero, sin, zrest], axis=1)
    two = lambda t: jnp.concatenate([t, t], axis=1)
    return two(cos_h), two(up_h), two(dn_h)


def _rope_kernel(t_ref, cos_ref, up_ref, dn_ref, *out_refs, split):
    cos, up, dn = cos_ref[...], up_ref[...], dn_ref[...]
    lane = lax.broadcasted_iota(jnp.int32, cos.shape, 1)
    for hd in range(t_ref.shape[1] // HEAD_DIM):
        sl = slice(hd * HEAD_DIM, (hd + 1) * HEAD_DIM)
        t = t_ref[:, sl]
        r = t * cos + pltpu.roll(t, HEAD_DIM - ROPE_HALF, 1) * up + pltpu.roll(t, ROPE_HALF, 1) * dn
        if split:
            out_refs[0][:, sl] = jnp.where(lane < DF_HALF, r, 0.0).astype(_BF16)
            out_refs[1][:, sl] = jnp.where(lane >= DF_HALF, r, 0.0).astype(_BF16)
        else:
            out_refs[0][:, sl] = r.astype(_BF16)


def _rope(p_dqk, tables, *, batch, seq, b_off, part, split):
    bt = min(ROPE_BT, seq)
    p3 = p_dqk.reshape(-1, seq, 2 * DF_W)
    tab = pl.BlockSpec((bt, HEAD_DIM), lambda b, i: (i, 0))
    out = pl.BlockSpec((None, bt, DF_W), lambda b, i: (b, i, 0))
    n_out = 2 if split else 1
    res = pl.pallas_call(
        functools.partial(_rope_kernel, split=split),
        grid=(batch, seq // bt),
        in_specs=[pl.BlockSpec((None, bt, DF_W), lambda b, i: (b + b_off, i, part)), tab, tab, tab],
        out_specs=[out] * n_out,
        out_shape=[jax.ShapeDtypeStruct((batch, seq, DF_W), _BF16)] * n_out,
        compiler_params=_cparams("parallel", "parallel"),
        name="rope_k" if split else "rope_q",
    )(p3, *tables)
    return res


def _diff_kernel(q_ref, ka_ref, kb_ref, vt_ref, lp_ref, ng_ref, o_ref, m_ref, acc_ref, sa_ref, sb_ref, *,
                 lambda_init, bk):
    nkb = ka_ref.shape[0] // bk
    q = q_ref[...]
    m_ref[...] = jnp.full_like(m_ref, -jnp.inf)
    acc_ref[...] = jnp.zeros_like(acc_ref)

    def scores(j, dst):
        start = pl.multiple_of(j * bk, bk)
        for mp, k_ref in enumerate((ka_ref, kb_ref)):
            dst[mp] = lax.dot_general(k_ref[pl.ds(start, bk), :], q, (((1,), (1,)), ((), ())),
                                      preferred_element_type=_F32)

    def accumulate(j, src):
        vt = vt_ref[j]
        for mp in range(2):
            s_t = src[mp]
            m_old = m_ref[mp]
            m_new = jnp.maximum(m_old, jnp.max(s_t, axis=0, keepdims=True))
            alpha = jnp.exp2(m_old - m_new)
            p_t = jnp.exp2(s_t - m_new).astype(_BF16)
            acc_ref[mp] = alpha * acc_ref[mp] + jnp.dot(vt, p_t, preferred_element_type=_F32)
            m_ref[mp] = m_new

    scores(0, sa_ref)

    def pair(jj, carry):
        j = 2 * jj
        scores(j + 1, sb_ref)
        accumulate(j, sa_ref)
        scores(j + 2, sa_ref)
        accumulate(j + 1, sb_ref)
        return carry

    lax.fori_loop(0, nkb // 2 - 1, pair, 0)
    scores(nkb - 1, sb_ref)
    accumulate(nkb - 2, sa_ref)
    accumulate(nkb - 1, sb_ref)

    lp = lp_ref[...]
    dots = jnp.sum(lp[0:1] * lp[1:2], axis=-1, keepdims=True), jnp.sum(lp[2:3] * lp[3:4], axis=-1, keepdims=True)
    lam = jnp.exp(dots[0]) - jnp.exp(dots[1]) + lambda_init
    a0, a1 = acc_ref[0], acc_ref[1]
    o_t = a0[:HEAD_DIM] / a0[HEAD_DIM:HEAD_DIM + 1] - lam * (a1[:HEAD_DIM] / a1[HEAD_DIM:HEAD_DIM + 1])
    o_t = o_t * lax.rsqrt(jnp.mean(o_t * o_t, axis=0, keepdims=True) + LN_EPS) * ng_ref[...] * (1.0 - lambda_init)
    o_ref[...] = o_t.T.astype(o_ref.dtype)


def _values_transposed(p_dv_rows, batch, seq, bk):
    nkb = seq // bk
    vt = p_dv_rows.reshape(batch, nkb, bk, DF_HEADS, HEAD_DIM).transpose(0, 3, 1, 4, 2)
    ones = jnp.ones((batch, DF_HEADS, nkb, 1, bk), vt.dtype)
    zeros = jnp.zeros((batch, DF_HEADS, nkb, DF_VT_ROWS - HEAD_DIM - 1, bk), vt.dtype)
    return jnp.concatenate([vt, ones, zeros], axis=3)


def _diff_mixer(q_rot, k_a, k_b, p_dv_rows, lam_p, norm_g, lambda_init, *, batch, seq):
    bq = min(DF_BQ, seq)
    bk = min(DF_BK, seq)
    nkb = seq // bk
    assert nkb % 2 == 0
    vt = _values_transposed(p_dv_rows, batch, seq, bk)
    full = pl.BlockSpec((None, seq, HEAD_DIM), lambda b, h, i: (b, 0, h))
    blk = pl.BlockSpec((None, bq, HEAD_DIM), lambda b, h, i: (b, i, h))
    y = pl.pallas_call(
        functools.partial(_diff_kernel, lambda_init=lambda_init, bk=bk),
        grid=(batch, DF_HEADS, seq // bq),
        in_specs=[blk, full, full,
                  pl.BlockSpec((None, None, nkb, DF_VT_ROWS, bk), lambda b, h, i: (b, h, 0, 0, 0)),
                  pl.BlockSpec((4, DF_HALF), lambda b, h, i: (0, 0)),
                  pl.BlockSpec((HEAD_DIM, 1), lambda b, h, i: (0, 0))],
        out_specs=blk,
        out_shape=jax.ShapeDtypeStruct((batch, seq, DF_W), _BF16),
        scratch_shapes=[pltpu.VMEM((2, 1, bq), _F32), pltpu.VMEM((2, DF_VT_ROWS, bq), _F32),
                        pltpu.VMEM((2, bk, bq), _F32), pltpu.VMEM((2, bk, bq), _F32)],
        compiler_params=_cparams("parallel", "parallel", "arbitrary"),
        name="diff_attention",
    )(q_rot, k_a, k_b, vt, lam_p.astype(_F32), norm_g.reshape(HEAD_DIM, 1).astype(_F32))
    return y.reshape(batch * seq, DF_W)


def _router_kernel(x_ref, w_ref, b_ref, idx_ref, gate_ref):
    logits = lax.dot_general(w_ref[...], x_ref[...], (((1,), (1,)), ((), ())),
                             preferred_element_type=_F32) + b_ref[...]
    mx = jnp.max(logits, axis=0, keepdims=True)
    ex = jnp.exp(logits - mx)
    probs = ex / jnp.sum(ex, axis=0, keepdims=True)
    e = [probs[j * N_GROUPS:(j + 1) * N_GROUPS] for j in range(EXPERTS_PER_GROUP)]
    hi01, lo01 = jnp.maximum(e[0], e[1]), jnp.minimum(e[0], e[1])
    hi23, lo23 = jnp.maximum(e[2], e[3]), jnp.minimum(e[2], e[3])
    gscore = jnp.maximum(hi01, hi23) + jnp.maximum(jnp.minimum(hi01, hi23), jnp.maximum(lo01, lo23))
    gid = lax.broadcasted_iota(jnp.int32, gscore.shape, 0).astype(_F32)
    best = jnp.min(jnp.where(gscore == jnp.max(gscore, axis=0, keepdims=True), gid, float(N_GROUPS)),
                   axis=0, keepdims=True)
    within = [jnp.sum(jnp.where(gid == best, ej, 0.0), axis=0, keepdims=True) for ej in e]

    def first_max(vals):
        top = functools.reduce(jnp.maximum, vals)
        pos = jnp.full(top.shape, float(EXPERTS_PER_GROUP), _F32)
        for j in reversed(range(EXPERTS_PER_GROUP)):
            pos = jnp.where(vals[j] == top, float(j), pos)
        return top, pos

    v1, i1 = first_max(within)
    v2, i2 = first_max([jnp.where(i1 == float(j), -1.0, within[j]) for j in range(EXPERTS_PER_GROUP)])
    base = best * EXPERTS_PER_GROUP
    idx_ref[0:1, :] = (base + i1).astype(jnp.int32)
    idx_ref[1:2, :] = (base + i2).astype(jnp.int32)
    tot = v1 + v2
    gate_ref[0:1, :] = v1 / tot
    gate_ref[1:2, :] = v2 / tot


def _route(xb, w_router, b_router):
    n, d = xb.shape
    tm = min(ROUTE_TM, n)
    perm = (jnp.arange(N_EXPERTS) % N_GROUPS) * EXPERTS_PER_GROUP + jnp.arange(N_EXPERTS) // N_GROUPS
    w_t = w_router.T[perm].astype(_BF16)
    b_c = b_router.astype(_F32)[perm].reshape(N_EXPERTS, 1)
    return pl.pallas_call(
        _router_kernel, grid=(n // tm,),
        in_specs=[pl.BlockSpec((tm, d), lambda i: (i, 0)),
                  pl.BlockSpec((N_EXPERTS, d), lambda i: (0, 0)),
                  pl.BlockSpec((N_EXPERTS, 1), lambda i: (0, 0))],
        out_specs=[pl.BlockSpec((TOP_K, tm), lambda i: (0, i))] * 2,
        out_shape=[jax.ShapeDtypeStruct((TOP_K, n), jnp.int32), jax.ShapeDtypeStruct((TOP_K, n), _F32)],
        compiler_params=_cparams("parallel"), name="router",
    )(xb, w_t, b_c)


def _expert_kernel(be_ref, nv_ref, x_ref, wg_ref, wu_ref, wd_ref, o_ref):
    blk, f = pl.program_id(0), pl.program_id(1)

    @pl.when(f == 0)
    def _():
        o_ref[...] = jnp.zeros_like(o_ref)

    @pl.when(blk < nv_ref[0])
    def _():
        x = x_ref[...]
        hg = jnp.dot(x, wg_ref[...], preferred_element_type=_F32)
        hu = jnp.dot(x, wu_ref[...], preferred_element_type=_F32)
        h = (hg * jax.nn.sigmoid(hg) * hu).astype(_BF16)
        o_ref[...] += jnp.dot(h, wd_ref[...], preferred_element_type=_F32)


def _experts(x_rows, block_expert, n_live, wg, wu, wd, *, layer, tm):
    p_rows, d = x_rows.shape
    nb = p_rows // tm
    ff = wg.shape[-1]
    tf = min(MOE_TF, ff)
    nf = ff // tf

    def fsel(b, f, nv):
        return jnp.where(b < nv[0], f, nf - 1)

    grid_spec = pltpu.PrefetchScalarGridSpec(
        num_scalar_prefetch=2,
        grid=(nb, nf),
        in_specs=[
            pl.BlockSpec((tm, d), lambda b, f, be, nv: (b, 0)),
            pl.BlockSpec((None, None, d, tf), lambda b, f, be, nv: (layer, be[b], 0, fsel(b, f, nv))),
            pl.BlockSpec((None, None, d, tf), lambda b, f, be, nv: (layer, be[b], 0, fsel(b, f, nv))),
            pl.BlockSpec((None, None, tf, d), lambda b, f, be, nv: (layer, be[b], fsel(b, f, nv), 0)),
        ],
        out_specs=pl.BlockSpec((tm, d), lambda b, f, be, nv: (b, 0)),
    )
    return pl.pallas_call(
        _expert_kernel, grid_spec=grid_spec,
        out_shape=jax.ShapeDtypeStruct((p_rows, d), _F32),
        compiler_params=_cparams("arbitrary", "arbitrary"), name="experts",
    )(block_expert, n_live, x_rows, wg, wu, wd)


def _dispatch_plan(idx, n_tokens, tm):
    a_tot = n_tokens * TOP_K
    nb = (a_tot + N_EXPERTS * (tm - 1) + tm - 1) // tm
    flat_e = idx.reshape(-1)
    onehot = (flat_e[:, None] == jnp.arange(N_EXPERTS)[None, :]).astype(jnp.int32)
    before = jnp.cumsum(onehot, axis=0) - onehot
    rank = jnp.sum(before * onehot, axis=1)
    counts = jnp.sum(onehot, axis=0)
    padded = (counts + tm - 1) // tm * tm
    pend = jnp.cumsum(padded)
    poff = pend - padded
    dest = poff[flat_e] + rank
    tok = jnp.tile(jnp.arange(n_tokens, dtype=jnp.int32), TOP_K)
    row_tok = jnp.full((nb * tm,), n_tokens, jnp.int32).at[dest].set(tok)
    block_expert = jnp.minimum(jnp.searchsorted(pend, jnp.arange(nb) * tm, side='right'),
                               N_EXPERTS - 1).astype(jnp.int32)
    n_live = (pend[-1] // tm).astype(jnp.int32).reshape(1)
    return row_tok, dest.reshape(TOP_K, n_tokens), block_expert, n_live


def _moe_block(x, xb, w_router, b_router, wg, wu, wd, layer, ln_g, ln_b, alpha):
    n, d = x.shape
    tm = min(MOE_TM, n)
    idx, gates = _route(xb, w_router, b_router)
    row_tok, pos, block_expert, n_live = _dispatch_plan(idx, n, tm)
    x_pad = jnp.concatenate([xb, jnp.zeros((1, d), xb.dtype)], axis=0)
    yb = _experts(x_pad[row_tok], block_expert, n_live, wg, wu, wd, layer=layer, tm=tm)
    return _combine_layer_norm(x, yb[pos[0]], yb[pos[1]], gates.T, ln_g, ln_b, alpha)


def _split_w_in(w_in_l, b_in_l):
    g0 = 4 * ML_W
    g1 = g0 + N_GATE_COLS
    w_main = jnp.concatenate([w_in_l[:, :g0], w_in_l[:, g1:]], axis=1).astype(_BF16)
    b_main = jnp.concatenate([b_in_l[:g0], b_in_l[g1:]]).astype(_F32).reshape(1, -1)
    pad = LANES - N_GATE_COLS
    w_gate = jnp.pad(w_in_l[:, g0:g1], ((0, 0), (0, pad))).astype(_BF16)
    b_gate = jnp.pad(b_in_l[g0:g1], (0, pad)).astype(_F32).reshape(1, -1)
    return w_main, b_main, w_gate, b_gate


def _trunk_layers(x, geoms, depth, w_in, b_in, ml_f_bias, ml_norm_g, na_rpb, df_lambda, df_norm_g, w_out,
                  ln_g, ln_b, w_router, b_router, w_gate, w_up, w_down):
    n, d = x.shape
    alpha = (2 * depth) ** 0.25
    xb = x.astype(_BF16)
    off_ml, off_na, off_dqk, off_dv = 0, 4 * ML_W, 4 * ML_W + 3 * NA_W, 4 * ML_W + 3 * NA_W + 2 * DF_W
    na_scale = jnp.ones((off_dv + DF_W,), _F32).at[off_na:off_na + NA_W].set(HEAD_DIM ** -0.5).reshape(1, -1)
    q_scale = DF_HALF ** -0.5 * math.log2(math.e)
    rope_tabs = {}
    for _, seq in geoms:
        if seq not in rope_tabs:
            cos, up, dn = _rope_tables(seq)
            rope_tabs[seq] = ((cos * q_scale, up * q_scale, dn * q_scale), (cos, up, dn))

    wg_b, wu_b, wd_b = w_gate.astype(_BF16), w_up.astype(_BF16), w_down.astype(_BF16)
    for l in range(depth):
        lambda_init = 0.8 - 0.6 * math.exp(-0.3 * l)
        w_main, b_main, w_g, b_g = _split_w_in(w_in[l], b_in[l])
        p_ml = _proj(xb, w_main, b_main, col_off=off_ml, ncols=4 * ML_W, out_dtype=_F32, name="proj_mlstm")
        p_na = _proj(xb, w_main, b_main, col_off=off_na, ncols=3 * NA_W, out_dtype=_BF16, scale=na_scale,
                     name="proj_natten")
        p_dqk = _proj(xb, w_main, b_main, col_off=off_dqk, ncols=2 * DF_W, out_dtype=_F32, name="proj_diff_qk")
        p_dv = _proj(xb, w_main, b_main, col_off=off_dv, ncols=DF_W, out_dtype=_BF16, name="proj_diff_v")
        p_g = _proj(xb, w_g, b_g, col_off=0, ncols=LANES, out_dtype=_F32, name="proj_gates")
        tables = _natten_tables(na_rpb[l])

        mixed = []
        tok0 = 0
        for batch, seq in geoms:
            b_off = tok0 // seq
            assert b_off * seq == tok0
            L = min(ML_CHUNK, seq)
            gts = p_g[tok0:tok0 + batch * seq, :N_GATE_COLS].reshape(batch, seq, 4, ML_HEADS)
            gts = gts.transpose(2, 0, 3, 1).reshape(4, batch, ML_HEADS, seq // L, 1, L)
            y_ml = _mlstm_mixer(p_ml, gts, ml_f_bias[l].astype(_F32), ml_norm_g[l].astype(_F32),
                                batch=batch, seq=seq, b_off=b_off)
            y_na = _natten_mixer(p_na, tables, batch=batch, seq=seq, b_off=b_off)
            tq, tk = rope_tabs[seq]
            (q_rot,) = _rope(p_dqk, tq, batch=batch, seq=seq, b_off=b_off, part=0, split=False)
            k_a, k_b = _rope(p_dqk, tk, batch=batch, seq=seq, b_off=b_off, part=1, split=True)
            y_df = _diff_mixer(q_rot, k_a, k_b, p_dv[tok0:tok0 + batch * seq], df_lambda[l], df_norm_g[l],
                               lambda_init, batch=batch, seq=seq)
            mixed.append(jnp.concatenate([y_ml, y_na, y_df], axis=1))
            tok0 += batch * seq
        y_mix = jnp.concatenate(mixed, axis=0) if len(mixed) > 1 else mixed[0]

        z = _proj(y_mix, w_out[l].astype(_BF16), None, col_off=0, ncols=d, out_dtype=_F32, res=x, res_alpha=alpha,
                  name="proj_out")
        x, xb = _layer_norm(z, ln_g[l, 0].astype(_F32), ln_b[l, 0].astype(_F32))
        x, xb = _moe_block(x, xb, w_router, b_router, wg_b, wu_b, wd_b, l,
                           ln_g[l, 1].astype(_F32), ln_b[l, 1].astype(_F32), alpha)
    return x


@jax.jit
def kernel(x_prompt, x_sample, w_in, b_in, ml_f_bias, ml_norm_g, na_rpb, df_lambda, df_norm_g, w_out, ln_g, ln_b,
           w_router, b_router, w_gate, w_up, w_down):
    d = x_prompt.shape[-1]
    geoms = [x_prompt.shape[:2], x_sample.shape[:2]]
    x = jnp.concatenate([x_prompt.reshape(-1, d), x_sample.reshape(-1, d)], axis=0)
    y = _trunk_layers(x, geoms, w_in.shape[0], w_in, b_in, ml_f_bias, ml_norm_g, na_rpb, df_lambda, df_norm_g,
                      w_out, ln_g, ln_b, w_router, b_router, w_gate, w_up, w_down)
    n0 = x_prompt.shape[0] * x_prompt.shape[1]
    return y[:n0].reshape(x_prompt.shape), y[n0:].reshape(x_sample.shape)
```

```python
import functools
import math

import jax
import jax.numpy as jnp
import numpy as np
from jax import lax
from jax.experimental import pallas as pl
from jax.experimental.pallas import tpu as pltpu

HEAD_DIM = 128
ML_HEADS = 8
NA_HEADS = 12
DF_HEADS = 12
ML_W = ML_HEADS * HEAD_DIM
NA_W = NA_HEADS * HEAD_DIM
DF_W = DF_HEADS * HEAD_DIM
DF_HALF = HEAD_DIM // 2
ROPE_DIM = DF_HALF // 4
ROPE_HALF = ROPE_DIM // 2
ROPE_THETA = 500000.0
GRID_W = 64
NA_WIN_R = 8
NA_WIN_C = 16
N_EXPERTS = 32
N_GROUPS = 8
EXPERTS_PER_GROUP = N_EXPERTS // N_GROUPS
TOP_K = 2
LN_EPS = 1e-5
N_GATE_COLS = 4 * ML_HEADS

LANES = 128
VMEM_LIMIT = 56 * 1024 * 1024

PROJ_TM = 1024
PROJ_TN = 512
LN_TM = 256
ML_CHUNK = 256
ML_HPS = 4
NA_QROWS = 8
NA_KROWS = 2 * NA_WIN_R
NA_HPS = 3
NA_BQ = NA_QROWS * GRID_W
NA_BK = NA_KROWS * GRID_W
DF_BQ = 1024
DF_BK = 512
DF_VT_ROWS = HEAD_DIM + 16
ROPE_BT = 512
ROUTE_TM = 512
MOE_TM = 512
MOE_TF = 256
NEG_BIG = -1e30

_BF16 = jnp.bfloat16
_F32 = jnp.float32


def _cparams(*sem):
    return pltpu.CompilerParams(dimension_semantics=sem, vmem_limit_bytes=VMEM_LIMIT)


def _proj_kernel(*refs, has_bias, has_scale, res_alpha):
    x_ref, w_ref = refs[:2]
    pos = 2
    acc = jnp.dot(x_ref[...], w_ref[...], preferred_element_type=_F32)
    if has_bias:
        acc = acc + refs[pos][...]
        pos += 1
    if has_scale:
        acc = acc * refs[pos][...]
        pos += 1
    if res_alpha is not None:
        acc = acc + res_alpha * refs[pos][...]
        pos += 1
    o_ref = refs[pos]
    o_ref[...] = acc.astype(o_ref.dtype)


def _proj(x, w, b, *, col_off, ncols, out_dtype, scale=None, res=None, res_alpha=None,
          tm=PROJ_TM, tn=PROJ_TN, name="proj"):
    m, kdim = x.shape
    tm = min(tm, m)
    tn = min(tn, ncols)
    assert m % tm == 0 and ncols % tn == 0 and col_off % tn == 0
    joff = col_off // tn
    in_specs = [
        pl.BlockSpec((tm, kdim), lambda i, j: (i, 0)),
        pl.BlockSpec((kdim, tn), lambda i, j: (0, j + joff)),
    ]
    args = [x, w]
    if b is not None:
        in_specs.append(pl.BlockSpec((1, tn), lambda i, j: (0, j + joff)))
        args.append(b)
    if scale is not None:
        in_specs.append(pl.BlockSpec((1, tn), lambda i, j: (0, j + joff)))
        args.append(scale)
    if res is not None:
        in_specs.append(pl.BlockSpec((tm, tn), lambda i, j: (i, j)))
        args.append(res)
    return pl.pallas_call(
        functools.partial(_proj_kernel, has_bias=b is not None, has_scale=scale is not None,
                          res_alpha=res_alpha),
        grid=(m // tm, ncols // tn),
        in_specs=in_specs,
        out_specs=pl.BlockSpec((tm, tn), lambda i, j: (i, j)),
        out_shape=jax.ShapeDtypeStruct((m, ncols), out_dtype),
        compiler_params=_cparams("parallel", "arbitrary"),
        name=name,
    )(*args)


def _ln_rows(z, g, b):
    mu = jnp.mean(z, axis=-1, keepdims=True)
    zc = z - mu
    var = jnp.mean(zc * zc, axis=-1, keepdims=True)
    return zc * lax.rsqrt(var + LN_EPS) * g + b


def _ln_kernel(z_ref, g_ref, b_ref, o_ref, ob_ref):
    y = _ln_rows(z_ref[...], g_ref[...], b_ref[...])
    o_ref[...] = y
    ob_ref[...] = y.astype(_BF16)


def _combine_ln_kernel(x_ref, y0_ref, y1_ref, gt_ref, g_ref, b_ref, o_ref, ob_ref, *, alpha):
    gt = gt_ref[...]
    moe = y0_ref[...] * gt[:, 0:1] + y1_ref[...] * gt[:, 1:2]
    y = _ln_rows(alpha * x_ref[...] + moe, g_ref[...], b_ref[...])
    o_ref[...] = y
    ob_ref[...] = y.astype(_BF16)


def _layer_norm(z, g, b):
    n, d = z.shape
    tm = min(LN_TM, n)
    row = pl.BlockSpec((tm, d), lambda i: (i, 0))
    vec = pl.BlockSpec((1, d), lambda i: (0, 0))
    return pl.pallas_call(
        _ln_kernel, grid=(n // tm,), in_specs=[row, vec, vec], out_specs=[row, row],
        out_shape=[jax.ShapeDtypeStruct((n, d), _F32), jax.ShapeDtypeStruct((n, d), _BF16)],
        compiler_params=_cparams("parallel"), name="layer_norm",
    )(z, g.reshape(1, d), b.reshape(1, d))


def _combine_layer_norm(x, y01, gates_t, g, b, alpha):
    n, d = x.shape
    tm = min(LN_TM, n)
    nblk = n // tm
    row = pl.BlockSpec((tm, d), lambda i: (i, 0))
    vec = pl.BlockSpec((1, d), lambda i: (0, 0))
    return pl.pallas_call(
        functools.partial(_combine_ln_kernel, alpha=alpha),
        grid=(nblk,),
        in_specs=[row, row, pl.BlockSpec((tm, d), lambda i: (i + nblk, 0)),
                  pl.BlockSpec((tm, TOP_K), lambda i: (i, 0)), vec, vec],
        out_specs=[row, row],
        out_shape=[jax.ShapeDtypeStruct((n, d), _F32), jax.ShapeDtypeStruct((n, d), _BF16)],
        compiler_params=_cparams("parallel"), name="combine_layer_norm",
    )(x, y01, y01, gates_t, g.reshape(1, d), b.reshape(1, d))


def _mlstm_kernel(fb_ref, q_ref, k_ref, v_ref, ig_ref, fg_ref, *rest, reverse):
    if reverse:
        hf_ref, og_ref, ng_ref, out_ref, c_ref, n_ref, m_ref = rest
    else:
        out_ref, c_ref, n_ref, m_ref = rest
    L = q_ref.shape[0]
    d = HEAD_DIM

    @pl.when(pl.program_id(2) == 0)
    def _():
        c_ref[...] = jnp.zeros_like(c_ref)
        n_ref[...] = jnp.zeros_like(n_ref)
        m_ref[...] = jnp.zeros_like(m_ref)

    rows = lax.broadcasted_iota(jnp.int32, (L, L), 0)
    cols = lax.broadcasted_iota(jnp.int32, (L, L), 1)
    eye = rows == cols
    if reverse:
        visible = cols >= rows
        csum = (rows >= cols).astype(_F32)
    else:
        visible = cols <= rows
        csum = (rows <= cols).astype(_F32)

    def to_col(r):
        return jnp.sum(jnp.where(eye, r, 0.0), axis=1, keepdims=True)

    for hh in range(ML_HPS):
        head = pl.program_id(1) * ML_HPS + hh
        sl = slice(hh * d, (hh + 1) * d)
        ii = ig_ref[hh]
        lf = jax.nn.log_sigmoid(fg_ref[hh] + fb_ref[1 if reverse else 0, head])
        b_r = jnp.dot(jnp.broadcast_to(lf, (8, L)), csum, precision=lax.Precision.HIGHEST,
                      preferred_element_type=_F32)[0:1]
        g = jnp.sum(lf, axis=-1, keepdims=True)
        b_c = to_col(b_r)
        a_r = g - b_r + ii
        m_loc = jnp.max(a_r, axis=-1, keepdims=True)
        w_c = to_col(jnp.exp(a_r - m_loc))

        q = q_ref[:, sl]
        k = k_ref[:, sl] * (d ** -0.5)
        qb = q.astype(_BF16)
        kb = k.astype(_BF16)
        vb = v_ref[:, sl].astype(_BF16)
        m_prev = m_ref[hh]

        dmat = jnp.where(visible, b_c - (b_r - ii), -jnp.inf)
        m_inter = b_c + m_prev
        m_row = jnp.maximum(m_inter, jnp.max(dmat, axis=1, keepdims=True))
        w_inter = jnp.exp(m_inter - m_row)
        qk = lax.dot_general(qb, kb, (((1,), (1,)), ((), ())), preferred_element_type=_F32)
        s = jnp.exp(dmat - m_row) * qk
        q_c = jnp.dot(qb, c_ref[hh].astype(_BF16), preferred_element_type=_F32)
        q_n = jnp.sum(q * n_ref[hh], axis=1, keepdims=True)
        num = w_inter * q_c + jnp.dot(s.astype(_BF16), vb, preferred_element_type=_F32)
        den = w_inter * q_n + jnp.sum(s, axis=1, keepdims=True)
        h = num / jnp.maximum(jnp.abs(den), jnp.exp(-m_row))

        kw = k * w_c
        c_loc = lax.dot_general(kw.astype(_BF16), vb, (((0,), (0,)), ((), ())), preferred_element_type=_F32)
        n_loc = jnp.sum(kw, axis=0, keepdims=True)
        gm = g + m_prev
        m_new = jnp.maximum(gm, m_loc)
        s_old = jnp.exp(gm - m_new)
        s_new = jnp.exp(m_loc - m_new)
        c_ref[hh] = s_old * c_ref[hh] + s_new * c_loc
        n_ref[hh] = s_old * n_ref[hh] + s_new * n_loc
        m_ref[hh] = m_new

        if reverse:
            h = h + hf_ref[:, sl]
            mu = jnp.mean(h, axis=-1, keepdims=True)
            hc = h - mu
            var = jnp.mean(hc * hc, axis=-1, keepdims=True)
            h = hc * lax.rsqrt(var + LN_EPS) * ng_ref[:, sl]
            out_ref[:, sl] = (h * jax.nn.sigmoid(og_ref[:, sl])).astype(out_ref.dtype)
        else:
            out_ref[:, sl] = h


def _mlstm_mixer(p_ml, gates, f_bias, norm_g, *, batch, seq, b_off):
    L = min(ML_CHUNK, seq)
    nc = seq // L
    p3 = p_ml.reshape(-1, seq, 4 * ML_W)
    hgroups = ML_HEADS // ML_HPS
    width = ML_HPS * HEAD_DIM
    grid = (batch, hgroups, nc)

    def run(reverse, extra_args, extra_specs, out_dtype):
        cidx = (lambda c: nc - 1 - c) if reverse else (lambda c: c)
        tok = lambda part: pl.BlockSpec((None, L, width),
                                        lambda b, h, c: (b + b_off, cidx(c), part * hgroups + h))
        gate = lambda which: pl.BlockSpec((None, None, ML_HPS, None, 1, L),
                                          lambda b, h, c: (which, b, h, cidx(c), 0, 0))
        gi, gf = (2, 3) if reverse else (0, 1)
        out_spec = pl.BlockSpec((None, L, width), lambda b, h, c: (b, cidx(c), h))
        in_specs = [pl.BlockSpec(memory_space=pltpu.SMEM), tok(0), tok(1), tok(2), gate(gi), gate(gf)]
        return pl.pallas_call(
            functools.partial(_mlstm_kernel, reverse=reverse),
            grid=grid,
            in_specs=in_specs + extra_specs(tok, out_spec),
            out_specs=out_spec,
            out_shape=jax.ShapeDtypeStruct((batch, seq, ML_W), out_dtype),
            scratch_shapes=[pltpu.VMEM((ML_HPS, HEAD_DIM, HEAD_DIM), _F32), pltpu.VMEM((ML_HPS, 1, HEAD_DIM), _F32),
                            pltpu.VMEM((ML_HPS, 1, 1), _F32)],
            compiler_params=_cparams("parallel", "parallel", "arbitrary"),
            name="mlstm_reverse" if reverse else "mlstm_forward",
        )(f_bias, p3, p3, p3, gates, gates, *extra_args)

    h_f = run(False, (), lambda tok, out_spec: [], _F32)
    ng = norm_g.reshape(1, ML_W)
    y = run(True, (h_f, p3, ng),
            lambda tok, out_spec: [out_spec, tok(3), pl.BlockSpec((1, width), lambda b, h, c: (0, h))],
            _BF16)
    return y.reshape(batch * seq, ML_W)


def _natten_tables(rpb):
    qr = np.arange(NA_QROWS)[:, None]
    kr = np.arange(NA_KROWS)[None, :]
    qc = np.arange(GRID_W)[:, None]
    kc = np.arange(GRID_W)[None, :]
    half = NA_WIN_R // 2
    cs = np.clip(qc - NA_WIN_C // 2, 0, GRID_W - NA_WIN_C)
    col_ok = (kc >= cs) & (kc < cs + NA_WIN_C)
    dc_idx = np.clip(kc - qc, -(NA_WIN_C - 1), NA_WIN_C - 1) + NA_WIN_C - 1
    col_bias = jnp.take(rpb.astype(_F32), jnp.asarray(dc_idx.reshape(-1)), axis=2)
    col_bias = col_bias.reshape(NA_HEADS, 2 * NA_WIN_R - 1, GRID_W, GRID_W)
    geoms = ((0, np.maximum(qr - half, 0)), (half, qr), (NA_QROWS, np.minimum(qr + half, NA_QROWS)))
    tabs = []
    for shift, first in geoms:
        dr_idx = np.clip(kr - qr - shift + NA_WIN_R - 1, 0, 2 * NA_WIN_R - 2)
        row_ok = (kr >= first) & (kr < first + NA_WIN_R)
        bias = jnp.take(col_bias, jnp.asarray(dr_idx.reshape(-1)), axis=1)
        bias = bias.reshape(NA_HEADS, NA_QROWS, NA_KROWS, GRID_W, GRID_W).transpose(0, 1, 3, 2, 4)
        ok = row_ok[:, None, :, None] & col_ok[None, :, None, :]
        tabs.append(jnp.where(jnp.asarray(ok)[None], bias, NEG_BIG).reshape(NA_HEADS, NA_BQ, NA_BK))
    return jnp.stack(tabs)


def _natten_kernel(q_ref, k_ref, v_ref, tab_ref, o_ref, *, nblk):
    i = pl.program_id(2)
    krow0 = jnp.clip(NA_QROWS * i - NA_WIN_R // 2, 0, nblk * NA_QROWS - NA_KROWS)
    start = pl.multiple_of(krow0 * GRID_W, NA_WIN_R // 2 * GRID_W)
    for hh in range(NA_HPS):
        sl = slice(hh * HEAD_DIM, (hh + 1) * HEAD_DIM)
        kk = k_ref[pl.ds(start, NA_BK), sl]
        vv = v_ref[pl.ds(start, NA_BK), sl]
        s = lax.dot_general(q_ref[:, sl], kk, (((1,), (1,)), ((), ())), preferred_element_type=_F32) + tab_ref[hh]
        m = jnp.max(s, axis=-1, keepdims=True)
        p = jnp.exp(s - m)
        l = jnp.sum(p, axis=-1, keepdims=True)
        o = jnp.dot(p.astype(_BF16), vv, preferred_element_type=_F32)
        o_ref[:, sl] = (o / l).astype(o_ref.dtype)


def _natten_mixer(p_na, tables, *, batch, seq, b_off):
    nblk = seq // NA_BQ
    assert nblk >= 2 and seq % NA_BQ == 0
    p3 = p_na.reshape(-1, seq, 3 * NA_W)

    def variant(i):
        return jnp.where(i == 0, 0, jnp.where(i == nblk - 1, 2, 1))

    hgroups = NA_HEADS // NA_HPS
    width = NA_HPS * HEAD_DIM
    full = lambda part: pl.BlockSpec((None, seq, width), lambda b, h, i: (b + b_off, 0, part * hgroups + h))
    y = pl.pallas_call(
        functools.partial(_natten_kernel, nblk=nblk),
        grid=(batch, hgroups, nblk),
        in_specs=[pl.BlockSpec((None, NA_BQ, width), lambda b, h, i: (b + b_off, i, h)),
                  full(1), full(2),
                  pl.BlockSpec((None, NA_HPS, NA_BQ, NA_BK), lambda b, h, i: (variant(i), h, 0, 0))],
        out_specs=pl.BlockSpec((None, NA_BQ, width), lambda b, h, i: (b, i, h)),
        out_shape=jax.ShapeDtypeStruct((batch, seq, NA_W), _BF16),
        compiler_params=_cparams("parallel", "parallel", "arbitrary"),
        name="natten",
    )(p3, p3, p3, tables)
    return y.reshape(batch * seq, NA_W)


def _rope_tables(seq):
    pos = jnp.arange(seq, dtype=_F32)
    inv_freq = jnp.float32(ROPE_THETA) ** (-jnp.arange(0, ROPE_DIM, 2, dtype=_F32) / ROPE_DIM)
    ang = pos[:, None] * inv_freq[None, :]
    cos, sin = jnp.cos(ang), jnp.sin(ang)
    one = jnp.ones((seq, DF_HALF - ROPE_DIM), _F32)
    zero = jnp.zeros((seq, ROPE_HALF), _F32)
    zrest = jnp.zeros((seq, DF_HALF - ROPE_DIM), _F32)
    cos_h = jnp.concatenate([cos, cos, one], axis=1)
    up_h = jnp.concatenate([-sin, zero, zrest], axis=1)
    dn_h = jnp.concatenate([zero, sin, zrest], axis=1)
    two = lambda t: jnp.concatenate([t, t], axis=1)
    return two(cos_h), two(up_h), two(dn_h)


def _rope_kernel(t_ref, cos_ref, up_ref, dn_ref, *out_refs, split):
    cos, up, dn = cos_ref[...], up_ref[...], dn_ref[...]
    lane = lax.broadcasted_iota(jnp.int32, cos.shape, 1)
    for hd in range(t_ref.shape[1] // HEAD_DIM):
        sl = slice(hd * HEAD_DIM, (hd + 1) * HEAD_DIM)
        t = t_ref[:, sl]
        r = t * cos + pltpu.roll(t, HEAD_DIM - ROPE_HALF, 1) * up + pltpu.roll(t, ROPE_HALF, 1) * dn
        if split:
            out_refs[0][:, sl] = jnp.where(lane < DF_HALF, r, 0.0).astype(_BF16)
            out_refs[1][:, sl] = jnp.where(lane >= DF_HALF, r, 0.0).astype(_BF16)
        else:
            out_refs[0][:, sl] = r.astype(_BF16)


def _rope(p_dqk, tables, *, batch, seq, b_off, part, split):
    bt = min(ROPE_BT, seq)
    p3 = p_dqk.reshape(-1, seq, 2 * DF_W)
    tab = pl.BlockSpec((bt, HEAD_DIM), lambda b, i: (i, 0))
    out = pl.BlockSpec((None, bt, DF_W), lambda b, i: (b, i, 0))
    n_out = 2 if split else 1
    res = pl.pallas_call(
        functools.partial(_rope_kernel, split=split),
        grid=(batch, seq // bt),
        in_specs=[pl.BlockSpec((None, bt, DF_W), lambda b, i: (b + b_off, i, part)), tab, tab, tab],
        out_specs=[out] * n_out,
        out_shape=[jax.ShapeDtypeStruct((batch, seq, DF_W), _BF16)] * n_out,
        compiler_params=_cparams("parallel", "parallel"),
        name="rope_k" if split else "rope_q",
    )(p3, *tables)
    return res


def _diff_kernel(q_ref, ka_ref, kb_ref, vt_ref, lp_ref, ng_ref, o_ref, m_ref, acc_ref, sa_ref, sb_ref, *,
                 lambda_init, bk):
    nkb = ka_ref.shape[0] // bk
    q = q_ref[...]
    m_ref[...] = jnp.full_like(m_ref, -jnp.inf)
    acc_ref[...] = jnp.zeros_like(acc_ref)

    def scores(j, dst):
        start = pl.multiple_of(j * bk, bk)
        for mp, k_ref in enumerate((ka_ref, kb_ref)):
            dst[mp] = lax.dot_general(k_ref[pl.ds(start, bk), :], q, (((1,), (1,)), ((), ())),
                                      preferred_element_type=_F32)

    def accumulate(j, src):
        vt = vt_ref[j]
        for mp in range(2):
            s_t = src[mp]
            m_old = m_ref[mp]
            m_new = jnp.maximum(m_old, jnp.max(s_t, axis=0, keepdims=True))
            alpha = jnp.exp2(m_old - m_new)
            p_t = jnp.exp2(s_t - m_new).astype(_BF16)
            acc_ref[mp] = alpha * acc_ref[mp] + jnp.dot(vt, p_t, preferred_element_type=_F32)
            m_ref[mp] = m_new

    scores(0, sa_ref)

    def pair(jj, carry):
        j = 2 * jj
        scores(j + 1, sb_ref)
        accumulate(j, sa_ref)
        scores(j + 2, sa_ref)
        accumulate(j + 1, sb_ref)
        return carry

    lax.fori_loop(0, nkb // 2 - 1, pair, 0)
    scores(nkb - 1, sb_ref)
    accumulate(nkb - 2, sa_ref)
    accumulate(nkb - 1, sb_ref)

    lp = lp_ref[...]
    dots = jnp.sum(lp[0:1] * lp[1:2], axis=-1, keepdims=True), jnp.sum(lp[2:3] * lp[3:4], axis=-1, keepdims=True)
    lam = jnp.exp(dots[0]) - jnp.exp(dots[1]) + lambda_init
    a0, a1 = acc_ref[0], acc_ref[1]
    o_t = a0[:HEAD_DIM] / a0[HEAD_DIM:HEAD_DIM + 1] - lam * (a1[:HEAD_DIM] / a1[HEAD_DIM:HEAD_DIM + 1])
    o_t = o_t * lax.rsqrt(jnp.mean(o_t * o_t, axis=0, keepdims=True) + LN_EPS) * ng_ref[...] * (1.0 - lambda_init)
    o_ref[...] = o_t.T.astype(o_ref.dtype)


def _values_transposed(p_dv_rows, batch, seq, bk):
    nkb = seq // bk
    vt = p_dv_rows.reshape(batch, nkb, bk, DF_HEADS, HEAD_DIM).transpose(0, 3, 1, 4, 2)
    ones = jnp.ones((batch, DF_HEADS, nkb, 1, bk), vt.dtype)
    zeros = jnp.zeros((batch, DF_HEADS, nkb, DF_VT_ROWS - HEAD_DIM - 1, bk), vt.dtype)
    return jnp.concatenate([vt, ones, zeros], axis=3)


def _diff_mixer(q_rot, k_a, k_b, p_dv_rows, lam_p, norm_g, lambda_init, *, batch, seq):
    bq = min(DF_BQ, seq)
    bk = min(DF_BK, seq)
    nkb = seq // bk
    assert nkb % 2 == 0
    vt = _values_transposed(p_dv_rows, batch, seq, bk)
    full = pl.BlockSpec((None, seq, HEAD_DIM), lambda b, h, i: (b, 0, h))
    blk = pl.BlockSpec((None, bq, HEAD_DIM), lambda b, h, i: (b, i, h))
    y = pl.pallas_call(
        functools.partial(_diff_kernel, lambda_init=lambda_init, bk=bk),
        grid=(batch, DF_HEADS, seq // bq),
        in_specs=[blk, full, full,
                  pl.BlockSpec((None, None, nkb, DF_VT_ROWS, bk), lambda b, h, i: (b, h, 0, 0, 0)),
                  pl.BlockSpec((4, DF_HALF), lambda b, h, i: (0, 0)),
                  pl.BlockSpec((HEAD_DIM, 1), lambda b, h, i: (0, 0))],
        out_specs=blk,
        out_shape=jax.ShapeDtypeStruct((batch, seq, DF_W), _BF16),
        scratch_shapes=[pltpu.VMEM((2, 1, bq), _F32), pltpu.VMEM((2, DF_VT_ROWS, bq), _F32),
                        pltpu.VMEM((2, bk, bq), _F32), pltpu.VMEM((2, bk, bq), _F32)],
        compiler_params=_cparams("parallel", "parallel", "arbitrary"),
        name="diff_attention",
    )(q_rot, k_a, k_b, vt, lam_p.astype(_F32), norm_g.reshape(HEAD_DIM, 1).astype(_F32))
    return y.reshape(batch * seq, DF_W)


def _router_kernel(x_ref, w_ref, b_ref, idx_ref, gate_ref):
    logits = lax.dot_general(w_ref[...], x_ref[...], (((1,), (1,)), ((), ())),
                             preferred_element_type=_F32) + b_ref[...]
    mx = jnp.max(logits, axis=0, keepdims=True)
    ex = jnp.exp(logits - mx)
    probs = ex / jnp.sum(ex, axis=0, keepdims=True)
    e = [probs[j * N_GROUPS:(j + 1) * N_GROUPS] for j in range(EXPERTS_PER_GROUP)]
    hi01, lo01 = jnp.maximum(e[0], e[1]), jnp.minimum(e[0], e[1])
    hi23, lo23 = jnp.maximum(e[2], e[3]), jnp.minimum(e[2], e[3])
    gscore = jnp.maximum(hi01, hi23) + jnp.maximum(jnp.minimum(hi01, hi23), jnp.maximum(lo01, lo23))
    gid = lax.broadcasted_iota(jnp.int32, gscore.shape, 0).astype(_F32)
    best = jnp.min(jnp.where(gscore == jnp.max(gscore, axis=0, keepdims=True), gid, float(N_GROUPS)),
                   axis=0, keepdims=True)
    within = [jnp.sum(jnp.where(gid == best, ej, 0.0), axis=0, keepdims=True) for ej in e]

    def first_max(vals):
        top = functools.reduce(jnp.maximum, vals)
        pos = jnp.full(top.shape, float(EXPERTS_PER_GROUP), _F32)
        for j in reversed(range(EXPERTS_PER_GROUP)):
            pos = jnp.where(vals[j] == top, float(j), pos)
        return top, pos

    v1, i1 = first_max(within)
    v2, i2 = first_max([jnp.where(i1 == float(j), -1.0, within[j]) for j in range(EXPERTS_PER_GROUP)])
    base = best * EXPERTS_PER_GROUP
    idx_ref[0:1, :] = (base + i1).astype(jnp.int32)
    idx_ref[1:2, :] = (base + i2).astype(jnp.int32)
    tot = v1 + v2
    gate_ref[0:1, :] = v1 / tot
    gate_ref[1:2, :] = v2 / tot


def _route(xb, w_router, b_router):
    n, d = xb.shape
    tm = min(ROUTE_TM, n)
    perm = (jnp.arange(N_EXPERTS) % N_GROUPS) * EXPERTS_PER_GROUP + jnp.arange(N_EXPERTS) // N_GROUPS
    w_t = w_router.T[perm].astype(_BF16)
    b_c = b_router.astype(_F32)[perm].reshape(N_EXPERTS, 1)
    return pl.pallas_call(
        _router_kernel, grid=(n // tm,),
        in_specs=[pl.BlockSpec((tm, d), lambda i: (i, 0)),
                  pl.BlockSpec((N_EXPERTS, d), lambda i: (0, 0)),
                  pl.BlockSpec((N_EXPERTS, 1), lambda i: (0, 0))],
        out_specs=[pl.BlockSpec((TOP_K, tm), lambda i: (0, i))] * 2,
        out_shape=[jax.ShapeDtypeStruct((TOP_K, n), jnp.int32), jax.ShapeDtypeStruct((TOP_K, n), _F32)],
        compiler_params=_cparams("parallel"), name="router",
    )(xb, w_t, b_c)


def _expert_kernel(be_ref, nv_ref, x_ref, wg_ref, wu_ref, wd_ref, o_ref, acc_ref):
    blk, f = pl.program_id(0), pl.program_id(1)

    @pl.when(f == 0)
    def _():
        acc_ref[...] = jnp.zeros_like(acc_ref)

    @pl.when(blk < nv_ref[0])
    def _():
        x = x_ref[...]
        hg = jnp.dot(x, wg_ref[...], preferred_element_type=_F32)
        hu = jnp.dot(x, wu_ref[...], preferred_element_type=_F32)
        h = (hg * jax.nn.sigmoid(hg) * hu).astype(_BF16)
        acc_ref[...] += jnp.dot(h, wd_ref[...], preferred_element_type=_F32)

    @pl.when(f == pl.num_programs(1) - 1)
    def _():
        o_ref[...] = acc_ref[...].astype(o_ref.dtype)


def _experts(x_rows, block_expert, n_live, wg, wu, wd, *, layer, tm):
    p_rows, d = x_rows.shape
    nb = p_rows // tm
    ff = wg.shape[-1]
    tf = min(MOE_TF, ff)
    nf = ff // tf

    def fsel(b, f, nv):
        return jnp.where(b < nv[0], f, nf - 1)

    grid_spec = pltpu.PrefetchScalarGridSpec(
        num_scalar_prefetch=2,
        grid=(nb, nf),
        in_specs=[
            pl.BlockSpec((tm, d), lambda b, f, be, nv: (b, 0)),
            pl.BlockSpec((None, None, d, tf), lambda b, f, be, nv: (layer, be[b], 0, fsel(b, f, nv))),
            pl.BlockSpec((None, None, d, tf), lambda b, f, be, nv: (layer, be[b], 0, fsel(b, f, nv))),
            pl.BlockSpec((None, None, tf, d), lambda b, f, be, nv: (layer, be[b], fsel(b, f, nv), 0)),
        ],
        out_specs=pl.BlockSpec((tm, d), lambda b, f, be, nv: (b, 0)),
        scratch_shapes=[pltpu.VMEM((tm, d), _F32)],
    )
    return pl.pallas_call(
        _expert_kernel, grid_spec=grid_spec,
        out_shape=jax.ShapeDtypeStruct((p_rows, d), _BF16),
        compiler_params=_cparams("arbitrary", "arbitrary"), name="experts",
    )(block_expert, n_live, x_rows, wg, wu, wd)


def _dispatch_plan(idx, n_tokens, tm):
    a_tot = n_tokens * TOP_K
    nb = (a_tot + N_EXPERTS * (tm - 1) + tm - 1) // tm
    flat_e = idx.reshape(-1)
    onehot = (flat_e[:, None] == jnp.arange(N_EXPERTS)[None, :]).astype(jnp.int32)
    before = jnp.cumsum(onehot, axis=0) - onehot
    rank = jnp.sum(before * onehot, axis=1)
    counts = jnp.sum(onehot, axis=0)
    padded = (counts + tm - 1) // tm * tm
    pend = jnp.cumsum(padded)
    poff = pend - padded
    dest = poff[flat_e] + rank
    tok = jnp.tile(jnp.arange(n_tokens, dtype=jnp.int32), TOP_K)
    row_tok = jnp.zeros((nb * tm,), jnp.int32).at[dest].set(tok)
    block_expert = jnp.minimum(jnp.searchsorted(pend, jnp.arange(nb) * tm, side='right'),
                               N_EXPERTS - 1).astype(jnp.int32)
    n_live = (pend[-1] // tm).astype(jnp.int32).reshape(1)
    return row_tok, dest.reshape(TOP_K, n_tokens), block_expert, n_live


def _moe_block(x, xb, w_router, b_router, wg, wu, wd, layer, ln_g, ln_b, alpha):
    n, d = x.shape
    tm = min(MOE_TM, n)
    idx, gates = _route(xb, w_router, b_router)
    row_tok, pos, block_expert, n_live = _dispatch_plan(idx, n, tm)
    yb = _experts(xb[row_tok], block_expert, n_live, wg, wu, wd, layer=layer, tm=tm)
    return _combine_layer_norm(x, yb[pos.reshape(-1)], gates.T, ln_g, ln_b, alpha)


def _split_w_in(w_in_l, b_in_l):
    g0 = 4 * ML_W
    g1 = g0 + N_GATE_COLS
    w_main = jnp.concatenate([w_in_l[:, :g0], w_in_l[:, g1:]], axis=1).astype(_BF16)
    b_main = jnp.concatenate([b_in_l[:g0], b_in_l[g1:]]).astype(_F32).reshape(1, -1)
    pad = LANES - N_GATE_COLS
    w_gate = jnp.pad(w_in_l[:, g0:g1], ((0, 0), (0, pad))).astype(_BF16)
    b_gate = jnp.pad(b_in_l[g0:g1], (0, pad)).astype(_F32).reshape(1, -1)
    return w_main, b_main, w_gate, b_gate


def _trunk_layers(x, geoms, depth, w_in, b_in, ml_f_bias, ml_norm_g, na_rpb, df_lambda, df_norm_g, w_out,
                  ln_g, ln_b, w_router, b_router, w_gate, w_up, w_down):
    n, d = x.shape
    alpha = (2 * depth) ** 0.25
    xb = x.astype(_BF16)
    off_ml, off_na, off_dqk, off_dv = 0, 4 * ML_W, 4 * ML_W + 3 * NA_W, 4 * ML_W + 3 * NA_W + 2 * DF_W
    na_scale = jnp.ones((off_dv + DF_W,), _F32).at[off_na:off_na + NA_W].set(HEAD_DIM ** -0.5).reshape(1, -1)
    q_scale = DF_HALF ** -0.5 * math.log2(math.e)
    rope_tabs = {}
    for _, seq in geoms:
        if seq not in rope_tabs:
            cos, up, dn = _rope_tables(seq)
            rope_tabs[seq] = ((cos * q_scale, up * q_scale, dn * q_scale), (cos, up, dn))

    wg_b, wu_b, wd_b = w_gate.astype(_BF16), w_up.astype(_BF16), w_down.astype(_BF16)
    for l in range(depth):
        lambda_init = 0.8 - 0.6 * math.exp(-0.3 * l)
        w_main, b_main, w_g, b_g = _split_w_in(w_in[l], b_in[l])
        p_ml = _proj(xb, w_main, b_main, col_off=off_ml, ncols=4 * ML_W, out_dtype=_F32, name="proj_mlstm")
        p_na = _proj(xb, w_main, b_main, col_off=off_na, ncols=3 * NA_W, out_dtype=_BF16, scale=na_scale,
                     name="proj_natten")
        p_dqk = _proj(xb, w_main, b_main, col_off=off_dqk, ncols=2 * DF_W, out_dtype=_F32, name="proj_diff_qk")
        p_dv = _proj(xb, w_main, b_main, col_off=off_dv, ncols=DF_W, out_dtype=_BF16, name="proj_diff_v")
        p_g = _proj(xb, w_g, b_g, col_off=0, ncols=LANES, out_dtype=_F32, name="proj_gates")
        tables = _natten_tables(na_rpb[l])

        mixed = []
        tok0 = 0
        for batch, seq in geoms:
            b_off = tok0 // seq
            assert b_off * seq == tok0
            L = min(ML_CHUNK, seq)
            gts = p_g[tok0:tok0 + batch * seq, :N_GATE_COLS].reshape(batch, seq, 4, ML_HEADS)
            gts = gts.transpose(2, 0, 3, 1).reshape(4, batch, ML_HEADS, seq // L, 1, L)
            y_ml = _mlstm_mixer(p_ml, gts, ml_f_bias[l].astype(_F32), ml_norm_g[l].astype(_F32),
                                batch=batch, seq=seq, b_off=b_off)
            y_na = _natten_mixer(p_na, tables, batch=batch, seq=seq, b_off=b_off)
            tq, tk = rope_tabs[seq]
            (q_rot,) = _rope(p_dqk, tq, batch=batch, seq=seq, b_off=b_off, part=0, split=False)
            k_a, k_b = _rope(p_dqk, tk, batch=batch, seq=seq, b_off=b_off, part=1, split=True)
            y_df = _diff_mixer(q_rot, k_a, k_b, p_dv[tok0:tok0 + batch * seq], df_lambda[l], df_norm_g[l],
                               lambda_init, batch=batch, seq=seq)
            mixed.append(jnp.concatenate([y_ml, y_na, y_df], axis=1))
            tok0 += batch * seq
        y_mix = jnp.concatenate(mixed, axis=0) if len(mixed) > 1 else mixed[0]

        z = _proj(y_mix, w_out[l].astype(_BF16), None, col_off=0, ncols=d, out_dtype=_F32, res=x, res_alpha=alpha,
                  name="proj_out")
        x, xb = _layer_norm(z, ln_g[l, 0].astype(_F32), ln_b[l, 0].astype(_F32))
        x, xb = _moe_block(x, xb, w_router, b_router, wg_b, wu_b, wd_b, l,
                           ln_g[l, 1].astype(_F32), ln_b[l, 1].astype(_F32), alpha)
    return x


@jax.jit
def kernel(x_prompt, x_sample, w_in, b_in, ml_f_bias, ml_norm_g, na_rpb, df_lambda, df_norm_g, w_out, ln_g, ln_b,
           w_router, b_router, w_gate, w_up, w_down):
    d = x_prompt.shape[-1]
    geoms = [x_prompt.shape[:2], x_sample.shape[:2]]
    x = jnp.concatenate([x_prompt.reshape(-1, d), x_sample.reshape(-1, d)], axis=0)
    y = _trunk_layers(x, geoms, w_in.shape[0], w_in, b_in, ml_f_bias, ml_norm_g, na_rpb, df_lambda, df_norm_g,
                      w_out, ln_g, ln_b, w_router, b_router, w_gate, w_up, w_down)
    n0 = x_prompt.shape[0] * x_prompt.shape[1]
    return y[:n0].reshape(x_prompt.shape), y[n0:].reshape(x_sample.shape)
```

```python
import functools
import math

import jax
import jax.numpy as jnp
import numpy as np
from jax import lax
from jax.experimental import pallas as pl
from jax.experimental.pallas import tpu as pltpu

HEAD_DIM = 128
ML_HEADS = 8
NA_HEADS = 12
DF_HEADS = 12
ML_W = ML_HEADS * HEAD_DIM
NA_W = NA_HEADS * HEAD_DIM
DF_W = DF_HEADS * HEAD_DIM
DF_HALF = HEAD_DIM // 2
ROPE_DIM = DF_HALF // 4
ROPE_HALF = ROPE_DIM // 2
ROPE_THETA = 500000.0
GRID_W = 64
NA_WIN_R = 8
NA_WIN_C = 16
N_EXPERTS = 32
N_GROUPS = 8
EXPERTS_PER_GROUP = N_EXPERTS // N_GROUPS
TOP_K = 2
LN_EPS = 1e-5
N_GATE_COLS = 4 * ML_HEADS

LANES = 128
VMEM_LIMIT = 56 * 1024 * 1024

PROJ_TM = 1024
PROJ_TN = 512
LN_TM = 256
ML_CHUNK = 256
ML_HPS = 4
NA_QROWS = 8
NA_KROWS = 2 * NA_WIN_R
NA_HPS = 3
NA_BQ = NA_QROWS * GRID_W
NA_BK = NA_KROWS * GRID_W
DF_BQ = 1024
DF_BK = 512
DF_VT_ROWS = HEAD_DIM + 16
ROPE_BT = 512
ROUTE_TM = 512
MOE_TM = 512
MOE_TF = 256
PLAN_CHUNK = 128
NEG_BIG = -1e30

_BF16 = jnp.bfloat16
_F32 = jnp.float32


def _cparams(*sem):
    return pltpu.CompilerParams(dimension_semantics=sem, vmem_limit_bytes=VMEM_LIMIT)


def _proj_kernel(*refs, has_bias, has_scale, res_alpha):
    x_ref, w_ref = refs[:2]
    pos = 2
    acc = jnp.dot(x_ref[...], w_ref[...], preferred_element_type=_F32)
    if has_bias:
        acc = acc + refs[pos][...]
        pos += 1
    if has_scale:
        acc = acc * refs[pos][...]
        pos += 1
    if res_alpha is not None:
        acc = acc + res_alpha * refs[pos][...]
        pos += 1
    o_ref = refs[pos]
    o_ref[...] = acc.astype(o_ref.dtype)


def _proj(x, w, b, *, col_off, ncols, out_dtype, scale=None, res=None, res_alpha=None,
          tm=PROJ_TM, tn=PROJ_TN, name="proj"):
    m, kdim = x.shape
    tm = min(tm, m)
    tn = min(tn, ncols)
    assert m % tm == 0 and ncols % tn == 0 and col_off % tn == 0
    joff = col_off // tn
    in_specs = [
        pl.BlockSpec((tm, kdim), lambda i, j: (i, 0)),
        pl.BlockSpec((kdim, tn), lambda i, j: (0, j + joff)),
    ]
    args = [x, w]
    if b is not None:
        in_specs.append(pl.BlockSpec((1, tn), lambda i, j: (0, j + joff)))
        args.append(b)
    if scale is not None:
        in_specs.append(pl.BlockSpec((1, tn), lambda i, j: (0, j + joff)))
        args.append(scale)
    if res is not None:
        in_specs.append(pl.BlockSpec((tm, tn), lambda i, j: (i, j)))
        args.append(res)
    return pl.pallas_call(
        functools.partial(_proj_kernel, has_bias=b is not None, has_scale=scale is not None,
                          res_alpha=res_alpha),
        grid=(m // tm, ncols // tn),
        in_specs=in_specs,
        out_specs=pl.BlockSpec((tm, tn), lambda i, j: (i, j)),
        out_shape=jax.ShapeDtypeStruct((m, ncols), out_dtype),
        compiler_params=_cparams("parallel", "arbitrary"),
        name=name,
    )(*args)


def _ln_rows(z, g, b):
    mu = jnp.mean(z, axis=-1, keepdims=True)
    zc = z - mu
    var = jnp.mean(zc * zc, axis=-1, keepdims=True)
    return zc * lax.rsqrt(var + LN_EPS) * g + b


def _ln_kernel(z_ref, g_ref, b_ref, o_ref, ob_ref):
    y = _ln_rows(z_ref[...], g_ref[...], b_ref[...])
    o_ref[...] = y
    ob_ref[...] = y.astype(_BF16)


def _combine_ln_kernel(x_ref, y0_ref, y1_ref, gt_ref, g_ref, b_ref, o_ref, ob_ref, *, alpha):
    gt = gt_ref[...]
    moe = y0_ref[...] * gt[:, 0:1] + y1_ref[...] * gt[:, 1:2]
    y = _ln_rows(alpha * x_ref[...] + moe, g_ref[...], b_ref[...])
    o_ref[...] = y
    ob_ref[...] = y.astype(_BF16)


def _layer_norm(z, g, b):
    n, d = z.shape
    tm = min(LN_TM, n)
    row = pl.BlockSpec((tm, d), lambda i: (i, 0))
    vec = pl.BlockSpec((1, d), lambda i: (0, 0))
    return pl.pallas_call(
        _ln_kernel, grid=(n // tm,), in_specs=[row, vec, vec], out_specs=[row, row],
        out_shape=[jax.ShapeDtypeStruct((n, d), _F32), jax.ShapeDtypeStruct((n, d), _BF16)],
        compiler_params=_cparams("parallel"), name="layer_norm",
    )(z, g.reshape(1, d), b.reshape(1, d))


def _combine_layer_norm(x, y01, gates_t, g, b, alpha):
    n, d = x.shape
    tm = min(LN_TM, n)
    nblk = n // tm
    row = pl.BlockSpec((tm, d), lambda i: (i, 0))
    vec = pl.BlockSpec((1, d), lambda i: (0, 0))
    return pl.pallas_call(
        functools.partial(_combine_ln_kernel, alpha=alpha),
        grid=(nblk,),
        in_specs=[row, row, pl.BlockSpec((tm, d), lambda i: (i + nblk, 0)),
                  pl.BlockSpec((tm, TOP_K), lambda i: (i, 0)), vec, vec],
        out_specs=[row, row],
        out_shape=[jax.ShapeDtypeStruct((n, d), _F32), jax.ShapeDtypeStruct((n, d), _BF16)],
        compiler_params=_cparams("parallel"), name="combine_layer_norm",
    )(x, y01, y01, gates_t, g.reshape(1, d), b.reshape(1, d))


def _mlstm_kernel(fb_ref, q_ref, k_ref, v_ref, ig_ref, fg_ref, *rest, reverse):
    if reverse:
        hf_ref, og_ref, ng_ref, out_ref, c_ref, n_ref, m_ref = rest
    else:
        out_ref, c_ref, n_ref, m_ref = rest
    L = q_ref.shape[0]
    d = HEAD_DIM

    @pl.when(pl.program_id(2) == 0)
    def _():
        c_ref[...] = jnp.zeros_like(c_ref)
        n_ref[...] = jnp.zeros_like(n_ref)
        m_ref[...] = jnp.zeros_like(m_ref)

    rows = lax.broadcasted_iota(jnp.int32, (L, L), 0)
    cols = lax.broadcasted_iota(jnp.int32, (L, L), 1)
    eye = rows == cols
    if reverse:
        visible = cols >= rows
        csum = (rows >= cols).astype(_F32)
    else:
        visible = cols <= rows
        csum = (rows <= cols).astype(_F32)

    def to_col(r):
        return jnp.sum(jnp.where(eye, r, 0.0), axis=1, keepdims=True)

    for hh in range(ML_HPS):
        head = pl.program_id(1) * ML_HPS + hh
        sl = slice(hh * d, (hh + 1) * d)
        ii = ig_ref[hh]
        lf = jax.nn.log_sigmoid(fg_ref[hh] + fb_ref[1 if reverse else 0, head])
        b_r = jnp.dot(jnp.broadcast_to(lf, (8, L)), csum, precision=lax.Precision.HIGHEST,
                      preferred_element_type=_F32)[0:1]
        g = jnp.sum(lf, axis=-1, keepdims=True)
        b_c = to_col(b_r)
        a_r = g - b_r + ii
        m_loc = jnp.max(a_r, axis=-1, keepdims=True)
        w_c = to_col(jnp.exp(a_r - m_loc))

        q = q_ref[:, sl]
        k = k_ref[:, sl] * (d ** -0.5)
        qb = q.astype(_BF16)
        kb = k.astype(_BF16)
        vb = v_ref[:, sl].astype(_BF16)
        m_prev = m_ref[hh]

        dmat = jnp.where(visible, b_c - (b_r - ii), -jnp.inf)
        m_inter = b_c + m_prev
        m_row = jnp.maximum(m_inter, jnp.max(dmat, axis=1, keepdims=True))
        w_inter = jnp.exp(m_inter - m_row)
        qk = lax.dot_general(qb, kb, (((1,), (1,)), ((), ())), preferred_element_type=_F32)
        s = jnp.exp(dmat - m_row) * qk
        q_c = jnp.dot(qb, c_ref[hh].astype(_BF16), preferred_element_type=_F32)
        q_n = jnp.sum(q * n_ref[hh], axis=1, keepdims=True)
        num = w_inter * q_c + jnp.dot(s.astype(_BF16), vb, preferred_element_type=_F32)
        den = w_inter * q_n + jnp.sum(s, axis=1, keepdims=True)
        h = num / jnp.maximum(jnp.abs(den), jnp.exp(-m_row))

        kw = k * w_c
        c_loc = lax.dot_general(kw.astype(_BF16), vb, (((0,), (0,)), ((), ())), preferred_element_type=_F32)
        n_loc = jnp.sum(kw, axis=0, keepdims=True)
        gm = g + m_prev
        m_new = jnp.maximum(gm, m_loc)
        s_old = jnp.exp(gm - m_new)
        s_new = jnp.exp(m_loc - m_new)
        c_ref[hh] = s_old * c_ref[hh] + s_new * c_loc
        n_ref[hh] = s_old * n_ref[hh] + s_new * n_loc
        m_ref[hh] = m_new

        if reverse:
            h = h + hf_ref[:, sl]
            mu = jnp.mean(h, axis=-1, keepdims=True)
            hc = h - mu
            var = jnp.mean(hc * hc, axis=-1, keepdims=True)
            h = hc * lax.rsqrt(var + LN_EPS) * ng_ref[:, sl]
            out_ref[:, sl] = (h * jax.nn.sigmoid(og_ref[:, sl])).astype(out_ref.dtype)
        else:
            out_ref[:, sl] = h


def _mlstm_mixer(p_ml, gates, f_bias, norm_g, *, batch, seq, b_off):
    L = min(ML_CHUNK, seq)
    nc = seq // L
    p3 = p_ml.reshape(-1, seq, 4 * ML_W)
    hgroups = ML_HEADS // ML_HPS
    width = ML_HPS * HEAD_DIM
    grid = (batch, hgroups, nc)

    def run(reverse, extra_args, extra_specs, out_dtype):
        cidx = (lambda c: nc - 1 - c) if reverse else (lambda c: c)
        tok = lambda part: pl.BlockSpec((None, L, width),
                                        lambda b, h, c: (b + b_off, cidx(c), part * hgroups + h))
        gate = lambda which: pl.BlockSpec((None, None, ML_HPS, None, 1, L),
                                          lambda b, h, c: (which, b, h, cidx(c), 0, 0))
        gi, gf = (2, 3) if reverse else (0, 1)
        out_spec = pl.BlockSpec((None, L, width), lambda b, h, c: (b, cidx(c), h))
        in_specs = [pl.BlockSpec(memory_space=pltpu.SMEM), tok(0), tok(1), tok(2), gate(gi), gate(gf)]
        return pl.pallas_call(
            functools.partial(_mlstm_kernel, reverse=reverse),
            grid=grid,
            in_specs=in_specs + extra_specs(tok, out_spec),
            out_specs=out_spec,
            out_shape=jax.ShapeDtypeStruct((batch, seq, ML_W), out_dtype),
            scratch_shapes=[pltpu.VMEM((ML_HPS, HEAD_DIM, HEAD_DIM), _F32), pltpu.VMEM((ML_HPS, 1, HEAD_DIM), _F32),
                            pltpu.VMEM((ML_HPS, 1, 1), _F32)],
            compiler_params=_cparams("parallel", "parallel", "arbitrary"),
            name="mlstm_reverse" if reverse else "mlstm_forward",
        )(f_bias, p3, p3, p3, gates, gates, *extra_args)

    h_f = run(False, (), lambda tok, out_spec: [], _F32)
    ng = norm_g.reshape(1, ML_W)
    y = run(True, (h_f, p3, ng),
            lambda tok, out_spec: [out_spec, tok(3), pl.BlockSpec((1, width), lambda b, h, c: (0, h))],
            _BF16)
    return y.reshape(batch * seq, ML_W)


def _natten_tables(rpb):
    qr = np.arange(NA_QROWS)[:, None]
    kr = np.arange(NA_KROWS)[None, :]
    qc = np.arange(GRID_W)[:, None]
    kc = np.arange(GRID_W)[None, :]
    half = NA_WIN_R // 2
    cs = np.clip(qc - NA_WIN_C // 2, 0, GRID_W - NA_WIN_C)
    col_ok = (kc >= cs) & (kc < cs + NA_WIN_C)
    dc_idx = np.clip(kc - qc, -(NA_WIN_C - 1), NA_WIN_C - 1) + NA_WIN_C - 1
    col_bias = jnp.take(rpb.astype(_F32), jnp.asarray(dc_idx.reshape(-1)), axis=2)
    col_bias = col_bias.reshape(NA_HEADS, 2 * NA_WIN_R - 1, GRID_W, GRID_W)
    geoms = ((0, np.maximum(qr - half, 0)), (half, qr), (NA_QROWS, np.minimum(qr + half, NA_QROWS)))
    tabs = []
    for shift, first in geoms:
        dr_idx = np.clip(kr - qr - shift + NA_WIN_R - 1, 0, 2 * NA_WIN_R - 2)
        row_ok = (kr >= first) & (kr < first + NA_WIN_R)
        bias = jnp.take(col_bias, jnp.asarray(dr_idx.reshape(-1)), axis=1)
        bias = bias.reshape(NA_HEADS, NA_QROWS, NA_KROWS, GRID_W, GRID_W).transpose(0, 1, 3, 2, 4)
        ok = row_ok[:, None, :, None] & col_ok[None, :, None, :]
        tabs.append(jnp.where(jnp.asarray(ok)[None], bias, NEG_BIG).reshape(NA_HEADS, NA_BQ, NA_BK))
    return jnp.stack(tabs)


def _natten_kernel(q_ref, k_ref, v_ref, tab_ref, o_ref, *, nblk):
    i = pl.program_id(2)
    krow0 = jnp.clip(NA_QROWS * i - NA_WIN_R // 2, 0, nblk * NA_QROWS - NA_KROWS)
    start = pl.multiple_of(krow0 * GRID_W, NA_WIN_R // 2 * GRID_W)
    for hh in range(NA_HPS):
        sl = slice(hh * HEAD_DIM, (hh + 1) * HEAD_DIM)
        kk = k_ref[pl.ds(start, NA_BK), sl]
        vv = v_ref[pl.ds(start, NA_BK), sl]
        s = lax.dot_general(q_ref[:, sl], kk, (((1,), (1,)), ((), ())), preferred_element_type=_F32) + tab_ref[hh]
        m = jnp.max(s, axis=-1, keepdims=True)
        p = jnp.exp(s - m)
        l = jnp.sum(p, axis=-1, keepdims=True)
        o = jnp.dot(p.astype(_BF16), vv, preferred_element_type=_F32)
        o_ref[:, sl] = (o / l).astype(o_ref.dtype)


def _natten_mixer(p_na, tables, *, batch, seq, b_off):
    nblk = seq // NA_BQ
    assert nblk >= 2 and seq % NA_BQ == 0
    p3 = p_na.reshape(-1, seq, 3 * NA_W)

    def variant(i):
        return jnp.where(i == 0, 0, jnp.where(i == nblk - 1, 2, 1))

    hgroups = NA_HEADS // NA_HPS
    width = NA_HPS * HEAD_DIM
    full = lambda part: pl.BlockSpec((None, seq, width), lambda b, h, i: (b + b_off, 0, part * hgroups + h))
    y = pl.pallas_call(
        functools.partial(_natten_kernel, nblk=nblk),
        grid=(batch, hgroups, nblk),
        in_specs=[pl.BlockSpec((None, NA_BQ, width), lambda b, h, i: (b + b_off, i, h)),
                  full(1), full(2),
                  pl.BlockSpec((None, NA_HPS, NA_BQ, NA_BK), lambda b, h, i: (variant(i), h, 0, 0))],
        out_specs=pl.BlockSpec((None, NA_BQ, width), lambda b, h, i: (b, i, h)),
        out_shape=jax.ShapeDtypeStruct((batch, seq, NA_W), _BF16),
        compiler_params=_cparams("parallel", "parallel", "arbitrary"),
        name="natten",
    )(p3, p3, p3, tables)
    return y.reshape(batch * seq, NA_W)


def _rope_tables(seq):
    pos = jnp.arange(seq, dtype=_F32)
    inv_freq = jnp.float32(ROPE_THETA) ** (-jnp.arange(0, ROPE_DIM, 2, dtype=_F32) / ROPE_DIM)
    ang = pos[:, None] * inv_freq[None, :]
    cos, sin = jnp.cos(ang), jnp.sin(ang)
    one = jnp.ones((seq, DF_HALF - ROPE_DIM), _F32)
    zero = jnp.zeros((seq, ROPE_HALF), _F32)
    zrest = jnp.zeros((seq, DF_HALF - ROPE_DIM), _F32)
    cos_h = jnp.concatenate([cos, cos, one], axis=1)
    up_h = jnp.concatenate([-sin, zero, zrest], axis=1)
    dn_h = jnp.concatenate([zero, sin, zrest], axis=1)
    two = lambda t: jnp.concatenate([t, t], axis=1)
    return two(cos_h), two(up_h), two(dn_h)


def _rope_kernel(t_ref, cos_ref, up_ref, dn_ref, *out_refs, split):
    cos, up, dn = cos_ref[...], up_ref[...], dn_ref[...]
    lane = lax.broadcasted_iota(jnp.int32, cos.shape, 1)
    for hd in range(t_ref.shape[1] // HEAD_DIM):
        sl = slice(hd * HEAD_DIM, (hd + 1) * HEAD_DIM)
        t = t_ref[:, sl]
        r = t * cos + pltpu.roll(t, HEAD_DIM - ROPE_HALF, 1) * up + pltpu.roll(t, ROPE_HALF, 1) * dn
        if split:
            out_refs[0][:, sl] = jnp.where(lane < DF_HALF, r, 0.0).astype(_BF16)
            out_refs[1][:, sl] = jnp.where(lane >= DF_HALF, r, 0.0).astype(_BF16)
        else:
            out_refs[0][:, sl] = r.astype(_BF16)


def _rope(p_dqk, tables, *, batch, seq, b_off, part, split):
    bt = min(ROPE_BT, seq)
    p3 = p_dqk.reshape(-1, seq, 2 * DF_W)
    tab = pl.BlockSpec((bt, HEAD_DIM), lambda b, i: (i, 0))
    out = pl.BlockSpec((None, bt, DF_W), lambda b, i: (b, i, 0))
    n_out = 2 if split else 1
    res = pl.pallas_call(
        functools.partial(_rope_kernel, split=split),
        grid=(batch, seq // bt),
        in_specs=[pl.BlockSpec((None, bt, DF_W), lambda b, i: (b + b_off, i, part)), tab, tab, tab],
        out_specs=[out] * n_out,
        out_shape=[jax.ShapeDtypeStruct((batch, seq, DF_W), _BF16)] * n_out,
        compiler_params=_cparams("parallel", "parallel"),
        name="rope_k" if split else "rope_q",
    )(p3, *tables)
    return res


def _diff_kernel(q_ref, ka_ref, kb_ref, vt_ref, lp_ref, ng_ref, o_ref, m_ref, acc_ref, sa_ref, sb_ref, *,
                 lambda_init, bk):
    nkb = ka_ref.shape[0] // bk
    q = q_ref[...]
    m_ref[...] = jnp.full_like(m_ref, -jnp.inf)
    acc_ref[...] = jnp.zeros_like(acc_ref)

    def scores(j, dst):
        start = pl.multiple_of(j * bk, bk)
        for mp, k_ref in enumerate((ka_ref, kb_ref)):
            dst[mp] = lax.dot_general(k_ref[pl.ds(start, bk), :], q, (((1,), (1,)), ((), ())),
                                      preferred_element_type=_F32)

    def accumulate(j, src):
        vt = vt_ref[j]
        for mp in range(2):
            s_t = src[mp]
            m_old = m_ref[mp]
            m_new = jnp.maximum(m_old, jnp.max(s_t, axis=0, keepdims=True))
            alpha = jnp.exp2(m_old - m_new)
            p_t = jnp.exp2(s_t - m_new).astype(_BF16)
            acc_ref[mp] = alpha * acc_ref[mp] + jnp.dot(vt, p_t, preferred_element_type=_F32)
            m_ref[mp] = m_new

    scores(0, sa_ref)

    def pair(jj, carry):
        j = 2 * jj
        scores(j + 1, sb_ref)
        accumulate(j, sa_ref)
        scores(j + 2, sa_ref)
        accumulate(j + 1, sb_ref)
        return carry

    lax.fori_loop(0, nkb // 2 - 1, pair, 0)
    scores(nkb - 1, sb_ref)
    accumulate(nkb - 2, sa_ref)
    accumulate(nkb - 1, sb_ref)

    lp = lp_ref[...]
    dots = jnp.sum(lp[0:1] * lp[1:2], axis=-1, keepdims=True), jnp.sum(lp[2:3] * lp[3:4], axis=-1, keepdims=True)
    lam = jnp.exp(dots[0]) - jnp.exp(dots[1]) + lambda_init
    a0, a1 = acc_ref[0], acc_ref[1]
    o_t = a0[:HEAD_DIM] / a0[HEAD_DIM:HEAD_DIM + 1] - lam * (a1[:HEAD_DIM] / a1[HEAD_DIM:HEAD_DIM + 1])
    o_t = o_t * lax.rsqrt(jnp.mean(o_t * o_t, axis=0, keepdims=True) + LN_EPS) * ng_ref[...] * (1.0 - lambda_init)
    o_ref[...] = o_t.T.astype(o_ref.dtype)


def _values_transposed(p_dv_rows, batch, seq, bk):
    nkb = seq // bk
    vt = p_dv_rows.reshape(batch, nkb, bk, DF_HEADS, HEAD_DIM).transpose(0, 3, 1, 4, 2)
    ones = jnp.ones((batch, DF_HEADS, nkb, 1, bk), vt.dtype)
    zeros = jnp.zeros((batch, DF_HEADS, nkb, DF_VT_ROWS - HEAD_DIM - 1, bk), vt.dtype)
    return jnp.concatenate([vt, ones, zeros], axis=3)


def _diff_mixer(q_rot, k_a, k_b, p_dv_rows, lam_p, norm_g, lambda_init, *, batch, seq):
    bq = min(DF_BQ, seq)
    bk = min(DF_BK, seq)
    nkb = seq // bk
    assert nkb % 2 == 0
    vt = _values_transposed(p_dv_rows, batch, seq, bk)
    full = pl.BlockSpec((None, seq, HEAD_DIM), lambda b, h, i: (b, 0, h))
    blk = pl.BlockSpec((None, bq, HEAD_DIM), lambda b, h, i: (b, i, h))
    y = pl.pallas_call(
        functools.partial(_diff_kernel, lambda_init=lambda_init, bk=bk),
        grid=(batch, DF_HEADS, seq // bq),
        in_specs=[blk, full, full,
                  pl.BlockSpec((None, None, nkb, DF_VT_ROWS, bk), lambda b, h, i: (b, h, 0, 0, 0)),
                  pl.BlockSpec((4, DF_HALF), lambda b, h, i: (0, 0)),
                  pl.BlockSpec((HEAD_DIM, 1), lambda b, h, i: (0, 0))],
        out_specs=blk,
        out_shape=jax.ShapeDtypeStruct((batch, seq, DF_W), _BF16),
        scratch_shapes=[pltpu.VMEM((2, 1, bq), _F32), pltpu.VMEM((2, DF_VT_ROWS, bq), _F32),
                        pltpu.VMEM((2, bk, bq), _F32), pltpu.VMEM((2, bk, bq), _F32)],
        compiler_params=_cparams("parallel", "parallel", "arbitrary"),
        name="diff_attention",
    )(q_rot, k_a, k_b, vt, lam_p.astype(_F32), norm_g.reshape(HEAD_DIM, 1).astype(_F32))
    return y.reshape(batch * seq, DF_W)


def _router_kernel(x_ref, w_ref, b_ref, idx_ref, gate_ref):
    logits = lax.dot_general(w_ref[...], x_ref[...], (((1,), (1,)), ((), ())),
                             preferred_element_type=_F32) + b_ref[...]
    mx = jnp.max(logits, axis=0, keepdims=True)
    ex = jnp.exp(logits - mx)
    probs = ex / jnp.sum(ex, axis=0, keepdims=True)
    e = [probs[j * N_GROUPS:(j + 1) * N_GROUPS] for j in range(EXPERTS_PER_GROUP)]
    hi01, lo01 = jnp.maximum(e[0], e[1]), jnp.minimum(e[0], e[1])
    hi23, lo23 = jnp.maximum(e[2], e[3]), jnp.minimum(e[2], e[3])
    gscore = jnp.maximum(hi01, hi23) + jnp.maximum(jnp.minimum(hi01, hi23), jnp.maximum(lo01, lo23))
    gid = lax.broadcasted_iota(jnp.int32, gscore.shape, 0).astype(_F32)
    best = jnp.min(jnp.where(gscore == jnp.max(gscore, axis=0, keepdims=True), gid, float(N_GROUPS)),
                   axis=0, keepdims=True)
    within = [jnp.sum(jnp.where(gid == best, ej, 0.0), axis=0, keepdims=True) for ej in e]

    def first_max(vals):
        top = functools.reduce(jnp.maximum, vals)
        pos = jnp.full(top.shape, float(EXPERTS_PER_GROUP), _F32)
        for j in reversed(range(EXPERTS_PER_GROUP)):
            pos = jnp.where(vals[j] == top, float(j), pos)
        return top, pos

    v1, i1 = first_max(within)
    v2, i2 = first_max([jnp.where(i1 == float(j), -1.0, within[j]) for j in range(EXPERTS_PER_GROUP)])
    base = best * EXPERTS_PER_GROUP
    idx_ref[0:1, :] = (base + i1).astype(jnp.int32)
    idx_ref[1:2, :] = (base + i2).astype(jnp.int32)
    tot = v1 + v2
    gate_ref[0:1, :] = v1 / tot
    gate_ref[1:2, :] = v2 / tot


def _route(xb, w_router, b_router):
    n, d = xb.shape
    tm = min(ROUTE_TM, n)
    perm = (jnp.arange(N_EXPERTS) % N_GROUPS) * EXPERTS_PER_GROUP + jnp.arange(N_EXPERTS) // N_GROUPS
    w_t = w_router.T[perm].astype(_BF16)
    b_c = b_router.astype(_F32)[perm].reshape(N_EXPERTS, 1)
    return pl.pallas_call(
        _router_kernel, grid=(n // tm,),
        in_specs=[pl.BlockSpec((tm, d), lambda i: (i, 0)),
                  pl.BlockSpec((N_EXPERTS, d), lambda i: (0, 0)),
                  pl.BlockSpec((N_EXPERTS, 1), lambda i: (0, 0))],
        out_specs=[pl.BlockSpec((TOP_K, tm), lambda i: (0, i))] * 2,
        out_shape=[jax.ShapeDtypeStruct((TOP_K, n), jnp.int32), jax.ShapeDtypeStruct((TOP_K, n), _F32)],
        compiler_params=_cparams("parallel"), name="router",
    )(xb, w_t, b_c)


def _expert_kernel(be_ref, nv_ref, x_ref, wg_ref, wu_ref, wd_ref, o_ref, acc_ref):
    blk, f = pl.program_id(0), pl.program_id(1)

    @pl.when(f == 0)
    def _():
        acc_ref[...] = jnp.zeros_like(acc_ref)

    @pl.when(blk < nv_ref[0])
    def _():
        x = x_ref[...]
        hg = jnp.dot(x, wg_ref[...].astype(_BF16), preferred_element_type=_F32)
        hu = jnp.dot(x, wu_ref[...].astype(_BF16), preferred_element_type=_F32)
        h = (hg * jax.nn.sigmoid(hg) * hu).astype(_BF16)
        acc_ref[...] += jnp.dot(h, wd_ref[...].astype(_BF16), preferred_element_type=_F32)

    @pl.when(f == pl.num_programs(1) - 1)
    def _():
        o_ref[...] = acc_ref[...].astype(o_ref.dtype)


def _experts(x_rows, block_expert, n_live, wg, wu, wd, *, layer, tm):
    p_rows, d = x_rows.shape
    nb = p_rows // tm
    ff = wg.shape[-1]
    tf = min(MOE_TF, ff)
    nf = ff // tf

    def fsel(b, f, nv):
        return jnp.where(b < nv[0], f, nf - 1)

    grid_spec = pltpu.PrefetchScalarGridSpec(
        num_scalar_prefetch=2,
        grid=(nb, nf),
        in_specs=[
            pl.BlockSpec((tm, d), lambda b, f, be, nv: (b, 0)),
            pl.BlockSpec((None, None, d, tf), lambda b, f, be, nv: (layer, be[b], 0, fsel(b, f, nv))),
            pl.BlockSpec((None, None, d, tf), lambda b, f, be, nv: (layer, be[b], 0, fsel(b, f, nv))),
            pl.BlockSpec((None, None, tf, d), lambda b, f, be, nv: (layer, be[b], fsel(b, f, nv), 0)),
        ],
        out_specs=pl.BlockSpec((tm, d), lambda b, f, be, nv: (b, 0)),
        scratch_shapes=[pltpu.VMEM((tm, d), _F32)],
    )
    return pl.pallas_call(
        _expert_kernel, grid_spec=grid_spec,
        out_shape=jax.ShapeDtypeStruct((p_rows, d), _BF16),
        compiler_params=_cparams("arbitrary", "arbitrary"), name="experts",
    )(block_expert, n_live, x_rows, wg, wu, wd)


def _dispatch_plan(idx, n_tokens, tm):
    a_tot = n_tokens * TOP_K
    nb = (a_tot + N_EXPERTS * (tm - 1) + tm - 1) // tm
    flat_e = idx.reshape(-1)
    chunk = min(PLAN_CHUNK, a_tot)
    assert a_tot % chunk == 0
    onehot = (flat_e[:, None] == jnp.arange(N_EXPERTS)[None, :]).astype(_F32).reshape(-1, chunk, N_EXPERTS)
    earlier = jnp.asarray(np.tril(np.ones((chunk, chunk), np.float32), -1))
    within = jnp.einsum('ij,cje->cie', earlier, onehot)
    totals = jnp.sum(onehot, axis=1)
    before = within + (jnp.cumsum(totals, axis=0) - totals)[:, None, :]
    rank = jnp.sum(before * onehot, axis=2).reshape(-1).astype(jnp.int32)
    counts = jnp.sum(totals, axis=0).astype(jnp.int32)
    padded = (counts + tm - 1) // tm * tm
    pend = jnp.cumsum(padded)
    poff = pend - padded
    dest = poff[flat_e] + rank
    tok = jnp.tile(jnp.arange(n_tokens, dtype=jnp.int32), TOP_K)
    row_tok = jnp.zeros((nb * tm,), jnp.int32).at[dest].set(tok)
    block_expert = jnp.minimum(jnp.searchsorted(pend, jnp.arange(nb) * tm, side='right'),
                               N_EXPERTS - 1).astype(jnp.int32)
    n_live = (pend[-1] // tm).astype(jnp.int32).reshape(1)
    return row_tok, dest.reshape(TOP_K, n_tokens), block_expert, n_live


def _moe_block(x, xb, w_router, b_router, wg, wu, wd, layer, ln_g, ln_b, alpha):
    n, d = x.shape
    tm = min(MOE_TM, n)
    idx, gates = _route(xb, w_router, b_router)
    row_tok, pos, block_expert, n_live = _dispatch_plan(idx, n, tm)
    yb = _experts(xb[row_tok], block_expert, n_live, wg, wu, wd, layer=layer, tm=tm)
    return _combine_layer_norm(x, yb[pos.reshape(-1)], gates.T, ln_g, ln_b, alpha)


def _split_w_in(w_in_l, b_in_l):
    g0 = 4 * ML_W
    g1 = g0 + N_GATE_COLS
    w_main = jnp.concatenate([w_in_l[:, :g0], w_in_l[:, g1:]], axis=1).astype(_BF16)
    b_main = jnp.concatenate([b_in_l[:g0], b_in_l[g1:]]).astype(_F32).reshape(1, -1)
    pad = LANES - N_GATE_COLS
    w_gate = jnp.pad(w_in_l[:, g0:g1], ((0, 0), (0, pad))).astype(_BF16)
    b_gate = jnp.pad(b_in_l[g0:g1], (0, pad)).astype(_F32).reshape(1, -1)
    return w_main, b_main, w_gate, b_gate


def _trunk_layers(x, geoms, depth, w_in, b_in, ml_f_bias, ml_norm_g, na_rpb, df_lambda, df_norm_g, w_out,
                  ln_g, ln_b, w_router, b_router, w_gate, w_up, w_down):
    n, d = x.shape
    alpha = (2 * depth) ** 0.25
    xb = x.astype(_BF16)
    off_ml, off_na, off_dqk, off_dv = 0, 4 * ML_W, 4 * ML_W + 3 * NA_W, 4 * ML_W + 3 * NA_W + 2 * DF_W
    na_scale = jnp.ones((off_dv + DF_W,), _F32).at[off_na:off_na + NA_W].set(HEAD_DIM ** -0.5).reshape(1, -1)
    q_scale = DF_HALF ** -0.5 * math.log2(math.e)
    rope_tabs = {}
    for _, seq in geoms:
        if seq not in rope_tabs:
            cos, up, dn = _rope_tables(seq)
            rope_tabs[seq] = ((cos * q_scale, up * q_scale, dn * q_scale), (cos, up, dn))

    wg_b, wu_b, wd_b = w_gate.astype(_BF16), w_up, w_down
    for l in range(depth):
        lambda_init = 0.8 - 0.6 * math.exp(-0.3 * l)
        w_main, b_main, w_g, b_g = _split_w_in(w_in[l], b_in[l])
        p_ml = _proj(xb, w_main, b_main, col_off=off_ml, ncols=4 * ML_W, out_dtype=_F32, name="proj_mlstm")
        p_na = _proj(xb, w_main, b_main, col_off=off_na, ncols=3 * NA_W, out_dtype=_BF16, scale=na_scale,
                     name="proj_natten")
        p_dqk = _proj(xb, w_main, b_main, col_off=off_dqk, ncols=2 * DF_W, out_dtype=_F32, name="proj_diff_qk")
        p_dv = _proj(xb, w_main, b_main, col_off=off_dv, ncols=DF_W, out_dtype=_BF16, name="proj_diff_v")
        p_g = _proj(xb, w_g, b_g, col_off=0, ncols=LANES, out_dtype=_F32, name="proj_gates")
        tables = _natten_tables(na_rpb[l])

        mixed = []
        tok0 = 0
        for batch, seq in geoms:
            b_off = tok0 // seq
            assert b_off * seq == tok0
            L = min(ML_CHUNK, seq)
            gts = p_g[tok0:tok0 + batch * seq, :N_GATE_COLS].reshape(batch, seq, 4, ML_HEADS)
            gts = gts.transpose(2, 0, 3, 1).reshape(4, batch, ML_HEADS, seq // L, 1, L)
            y_ml = _mlstm_mixer(p_ml, gts, ml_f_bias[l].astype(_F32), ml_norm_g[l].astype(_F32),
                                batch=batch, seq=seq, b_off=b_off)
            y_na = _natten_mixer(p_na, tables, batch=batch, seq=seq, b_off=b_off)
            tq, tk = rope_tabs[seq]
            (q_rot,) = _rope(p_dqk, tq, batch=batch, seq=seq, b_off=b_off, part=0, split=False)
            k_a, k_b = _rope(p_dqk, tk, batch=batch, seq=seq, b_off=b_off, part=1, split=True)
            y_df = _diff_mixer(q_rot, k_a, k_b, p_dv[tok0:tok0 + batch * seq], df_lambda[l], df_norm_g[l],
                               lambda_init, batch=batch, seq=seq)
            mixed.append(jnp.concatenate([y_ml, y_na, y_df], axis=1))
            tok0 += batch * seq
        y_mix = jnp.concatenate(mixed, axis=0) if len(mixed) > 1 else mixed[0]

        z = _proj(y_mix, w_out[l].astype(_BF16), None, col_off=0, ncols=d, out_dtype=_F32, res=x, res_alpha=alpha,
                  name="proj_out")
        x, xb = _layer_norm(z, ln_g[l, 0].astype(_F32), ln_b[l, 0].astype(_F32))
        x, xb = _moe_block(x, xb, w_router, b_router, wg_b, wu_b, wd_b, l,
                           ln_g[l, 1].astype(_F32), ln_b[l, 1].astype(_F32), alpha)
    return x


@jax.jit
def kernel(x_prompt, x_sample, w_in, b_in, ml_f_bias, ml_norm_g, na_rpb, df_lambda, df_norm_g, w_out, ln_g, ln_b,
           w_router, b_router, w_gate, w_up, w_down):
    d = x_prompt.shape[-1]
    geoms = [x_prompt.shape[:2], x_sample.shape[:2]]
    x = jnp.concatenate([x_prompt.reshape(-1, d), x_sample.reshape(-1, d)], axis=0)
    y = _trunk_layers(x, geoms, w_in.shape[0], w_in, b_in, ml_f_bias, ml_norm_g, na_rpb, df_lambda, df_norm_g,
                      w_out, ln_g, ln_b, w_router, b_router, w_gate, w_up, w_down)
    n0 = x_prompt.shape[0] * x_prompt.shape[1]
    return y[:n0].reshape(x_prompt.shape), y[n0:].reshape(x_sample.shape)
```

```python
import functools
import math

import jax
import jax.numpy as jnp
import numpy as np
from jax import lax
from jax.experimental import pallas as pl
from jax.experimental.pallas import tpu as pltpu

HEAD_DIM = 128
ML_HEADS = 8
NA_HEADS = 12
DF_HEADS = 12
ML_W = ML_HEADS * HEAD_DIM
NA_W = NA_HEADS * HEAD_DIM
DF_W = DF_HEADS * HEAD_DIM
DF_HALF = HEAD_DIM // 2
ROPE_DIM = DF_HALF // 4
ROPE_HALF = ROPE_DIM // 2
ROPE_THETA = 500000.0
GRID_W = 64
NA_WIN_R = 8
NA_WIN_C = 16
N_EXPERTS = 32
N_GROUPS = 8
EXPERTS_PER_GROUP = N_EXPERTS // N_GROUPS
TOP_K = 2
LN_EPS = 1e-5
N_GATE_COLS = 4 * ML_HEADS

LANES = 128
VMEM_LIMIT = 56 * 1024 * 1024

PROJ_TM = 1024
PROJ_TN = 512
LN_TM = 256
ML_CHUNK = 256
ML_HPS = 4
NA_QROWS = 8
NA_KROWS = 2 * NA_WIN_R
NA_HPS = 3
NA_BQ = NA_QROWS * GRID_W
NA_BK = NA_KROWS * GRID_W
DF_BQ = 1024
DF_BK = 512
DF_VT_ROWS = HEAD_DIM + 16
ROPE_BT = 512
ROUTE_TM = 512
MOE_TM = 512
MOE_TF = 256
PLAN_CHUNK = 128
NEG_BIG = -1e30

_BF16 = jnp.bfloat16
_F32 = jnp.float32


def _cparams(*sem):
    return pltpu.CompilerParams(dimension_semantics=sem, vmem_limit_bytes=VMEM_LIMIT)


def _proj_kernel(*refs, has_bias, has_scale, res_alpha):
    x_ref, w_ref = refs[:2]
    pos = 2
    acc = jnp.dot(x_ref[...], w_ref[...], preferred_element_type=_F32)
    if has_bias:
        acc = acc + refs[pos][...]
        pos += 1
    if has_scale:
        acc = acc * refs[pos][...]
        pos += 1
    if res_alpha is not None:
        acc = acc + res_alpha * refs[pos][...]
        pos += 1
    o_ref = refs[pos]
    o_ref[...] = acc.astype(o_ref.dtype)


def _proj(x, w, b, *, col_off, ncols, out_dtype, scale=None, res=None, res_alpha=None,
          tm=PROJ_TM, tn=PROJ_TN, name="proj"):
    m, kdim = x.shape
    tm = min(tm, m)
    tn = min(tn, ncols)
    assert m % tm == 0 and ncols % tn == 0 and col_off % tn == 0
    joff = col_off // tn
    in_specs = [
        pl.BlockSpec((tm, kdim), lambda i, j: (i, 0)),
        pl.BlockSpec((kdim, tn), lambda i, j: (0, j + joff)),
    ]
    args = [x, w]
    if b is not None:
        in_specs.append(pl.BlockSpec((1, tn), lambda i, j: (0, j + joff)))
        args.append(b)
    if scale is not None:
        in_specs.append(pl.BlockSpec((1, tn), lambda i, j: (0, j + joff)))
        args.append(scale)
    if res is not None:
        in_specs.append(pl.BlockSpec((tm, tn), lambda i, j: (i, j)))
        args.append(res)
    return pl.pallas_call(
        functools.partial(_proj_kernel, has_bias=b is not None, has_scale=scale is not None,
                          res_alpha=res_alpha),
        grid=(m // tm, ncols // tn),
        in_specs=in_specs,
        out_specs=pl.BlockSpec((tm, tn), lambda i, j: (i, j)),
        out_shape=jax.ShapeDtypeStruct((m, ncols), out_dtype),
        compiler_params=_cparams("parallel", "arbitrary"),
        name=name,
    )(*args)


def _ln_rows(z, g, b):
    mu = jnp.mean(z, axis=-1, keepdims=True)
    zc = z - mu
    var = jnp.mean(zc * zc, axis=-1, keepdims=True)
    return zc * lax.rsqrt(var + LN_EPS) * g + b


def _ln_kernel(z_ref, g_ref, b_ref, o_ref, ob_ref):
    y = _ln_rows(z_ref[...], g_ref[...], b_ref[...])
    o_ref[...] = y
    ob_ref[...] = y.astype(_BF16)


def _combine_ln_kernel(x_ref, y0_ref, y1_ref, gt_ref, g_ref, b_ref, o_ref, ob_ref, *, alpha):
    gt = gt_ref[...]
    moe = y0_ref[...] * gt[:, 0:1] + y1_ref[...] * gt[:, 1:2]
    y = _ln_rows(alpha * x_ref[...] + moe, g_ref[...], b_ref[...])
    o_ref[...] = y
    ob_ref[...] = y.astype(_BF16)


def _layer_norm(z, g, b):
    n, d = z.shape
    tm = min(LN_TM, n)
    row = pl.BlockSpec((tm, d), lambda i: (i, 0))
    vec = pl.BlockSpec((1, d), lambda i: (0, 0))
    return pl.pallas_call(
        _ln_kernel, grid=(n // tm,), in_specs=[row, vec, vec], out_specs=[row, row],
        out_shape=[jax.ShapeDtypeStruct((n, d), _F32), jax.ShapeDtypeStruct((n, d), _BF16)],
        compiler_params=_cparams("parallel"), name="layer_norm",
    )(z, g.reshape(1, d), b.reshape(1, d))


def _combine_layer_norm(x, y01, gates_t, g, b, alpha):
    n, d = x.shape
    tm = min(LN_TM, n)
    nblk = n // tm
    row = pl.BlockSpec((tm, d), lambda i: (i, 0))
    vec = pl.BlockSpec((1, d), lambda i: (0, 0))
    return pl.pallas_call(
        functools.partial(_combine_ln_kernel, alpha=alpha),
        grid=(nblk,),
        in_specs=[row, row, pl.BlockSpec((tm, d), lambda i: (i + nblk, 0)),
                  pl.BlockSpec((tm, TOP_K), lambda i: (i, 0)), vec, vec],
        out_specs=[row, row],
        out_shape=[jax.ShapeDtypeStruct((n, d), _F32), jax.ShapeDtypeStruct((n, d), _BF16)],
        compiler_params=_cparams("parallel"), name="combine_layer_norm",
    )(x, y01, y01, gates_t, g.reshape(1, d), b.reshape(1, d))


def _mlstm_kernel(fb_ref, q_ref, k_ref, v_ref, ig_ref, fg_ref, *rest, reverse):
    if reverse:
        hf_ref, og_ref, ng_ref, out_ref, c_ref, n_ref, m_ref = rest
    else:
        out_ref, c_ref, n_ref, m_ref = rest
    L = q_ref.shape[0]
    d = HEAD_DIM

    @pl.when(pl.program_id(2) == 0)
    def _():
        c_ref[...] = jnp.zeros_like(c_ref)
        n_ref[...] = jnp.zeros_like(n_ref)
        m_ref[...] = jnp.zeros_like(m_ref)

    rows = lax.broadcasted_iota(jnp.int32, (L, L), 0)
    cols = lax.broadcasted_iota(jnp.int32, (L, L), 1)
    eye = rows == cols
    if reverse:
        visible = cols >= rows
        csum = (rows >= cols).astype(_F32)
    else:
        visible = cols <= rows
        csum = (rows <= cols).astype(_F32)

    def to_col(r):
        return jnp.sum(jnp.where(eye, r, 0.0), axis=1, keepdims=True)

    for hh in range(ML_HPS):
        head = pl.program_id(1) * ML_HPS + hh
        sl = slice(hh * d, (hh + 1) * d)
        ii = ig_ref[hh]
        lf = jax.nn.log_sigmoid(fg_ref[hh] + fb_ref[1 if reverse else 0, head])
        b_r = jnp.dot(jnp.broadcast_to(lf, (8, L)), csum, precision=lax.Precision.HIGHEST,
                      preferred_element_type=_F32)[0:1]
        g = jnp.sum(lf, axis=-1, keepdims=True)
        b_c = to_col(b_r)
        a_r = g - b_r + ii
        m_loc = jnp.max(a_r, axis=-1, keepdims=True)
        w_c = to_col(jnp.exp(a_r - m_loc))

        q = q_ref[:, sl]
        k = k_ref[:, sl] * (d ** -0.5)
        qb = q.astype(_BF16)
        kb = k.astype(_BF16)
        vb = v_ref[:, sl].astype(_BF16)
        m_prev = m_ref[hh]

        dmat = jnp.where(visible, b_c - (b_r - ii), -jnp.inf)
        m_inter = b_c + m_prev
        m_row = jnp.maximum(m_inter, jnp.max(dmat, axis=1, keepdims=True))
        w_inter = jnp.exp(m_inter - m_row)
        qk = lax.dot_general(qb, kb, (((1,), (1,)), ((), ())), preferred_element_type=_F32)
        s = jnp.exp(dmat - m_row) * qk
        q_c = jnp.dot(qb, c_ref[hh].astype(_BF16), preferred_element_type=_F32)
        q_n = jnp.sum(q * n_ref[hh], axis=1, keepdims=True)
        num = w_inter * q_c + jnp.dot(s.astype(_BF16), vb, preferred_element_type=_F32)
        den = w_inter * q_n + jnp.sum(s, axis=1, keepdims=True)
        h = num / jnp.maximum(jnp.abs(den), jnp.exp(-m_row))

        kw = k * w_c
        c_loc = lax.dot_general(kw.astype(_BF16), vb, (((0,), (0,)), ((), ())), preferred_element_type=_F32)
        n_loc = jnp.sum(kw, axis=0, keepdims=True)
        gm = g + m_prev
        m_new = jnp.maximum(gm, m_loc)
        s_old = jnp.exp(gm - m_new)
        s_new = jnp.exp(m_loc - m_new)
        c_ref[hh] = s_old * c_ref[hh] + s_new * c_loc
        n_ref[hh] = s_old * n_ref[hh] + s_new * n_loc
        m_ref[hh] = m_new

        if reverse:
            h = h + hf_ref[:, sl]
            mu = jnp.mean(h, axis=-1, keepdims=True)
            hc = h - mu
            var = jnp.mean(hc * hc, axis=-1, keepdims=True)
            h = hc * lax.rsqrt(var + LN_EPS) * ng_ref[:, sl]
            out_ref[:, sl] = (h * jax.nn.sigmoid(og_ref[:, sl])).astype(out_ref.dtype)
        else:
            out_ref[:, sl] = h


def _mlstm_mixer(p_ml, gates, f_bias, norm_g, *, batch, seq, b_off):
    L = min(ML_CHUNK, seq)
    nc = seq // L
    p3 = p_ml.reshape(-1, seq, 4 * ML_W)
    hgroups = ML_HEADS // ML_HPS
    width = ML_HPS * HEAD_DIM
    grid = (batch, hgroups, nc)

    def run(reverse, extra_args, extra_specs, out_dtype):
        cidx = (lambda c: nc - 1 - c) if reverse else (lambda c: c)
        tok = lambda part: pl.BlockSpec((None, L, width),
                                        lambda b, h, c: (b + b_off, cidx(c), part * hgroups + h))
        gate = lambda which: pl.BlockSpec((None, None, ML_HPS, None, 1, L),
                                          lambda b, h, c: (which, b, h, cidx(c), 0, 0))
        gi, gf = (2, 3) if reverse else (0, 1)
        out_spec = pl.BlockSpec((None, L, width), lambda b, h, c: (b, cidx(c), h))
        in_specs = [pl.BlockSpec(memory_space=pltpu.SMEM), tok(0), tok(1), tok(2), gate(gi), gate(gf)]
        return pl.pallas_call(
            functools.partial(_mlstm_kernel, reverse=reverse),
            grid=grid,
            in_specs=in_specs + extra_specs(tok, out_spec),
            out_specs=out_spec,
            out_shape=jax.ShapeDtypeStruct((batch, seq, ML_W), out_dtype),
            scratch_shapes=[pltpu.VMEM((ML_HPS, HEAD_DIM, HEAD_DIM), _F32), pltpu.VMEM((ML_HPS, 1, HEAD_DIM), _F32),
                            pltpu.VMEM((ML_HPS, 1, 1), _F32)],
            compiler_params=_cparams("parallel", "parallel", "arbitrary"),
            name="mlstm_reverse" if reverse else "mlstm_forward",
        )(f_bias, p3, p3, p3, gates, gates, *extra_args)

    h_f = run(False, (), lambda tok, out_spec: [], _F32)
    ng = norm_g.reshape(1, ML_W)
    y = run(True, (h_f, p3, ng),
            lambda tok, out_spec: [out_spec, tok(3), pl.BlockSpec((1, width), lambda b, h, c: (0, h))],
            _BF16)
    return y.reshape(batch * seq, ML_W)


def _natten_tables(rpb):
    qr = np.arange(NA_QROWS)[:, None]
    kr = np.arange(NA_KROWS)[None, :]
    qc = np.arange(GRID_W)[:, None]
    kc = np.arange(GRID_W)[None, :]
    half = NA_WIN_R // 2
    cs = np.clip(qc - NA_WIN_C // 2, 0, GRID_W - NA_WIN_C)
    col_ok = (kc >= cs) & (kc < cs + NA_WIN_C)
    dc_idx = np.clip(kc - qc, -(NA_WIN_C - 1), NA_WIN_C - 1) + NA_WIN_C - 1
    col_bias = jnp.take(rpb.astype(_F32), jnp.asarray(dc_idx.reshape(-1)), axis=2)
    col_bias = col_bias.reshape(NA_HEADS, 2 * NA_WIN_R - 1, GRID_W, GRID_W)
    geoms = ((0, np.maximum(qr - half, 0)), (half, qr), (NA_QROWS, np.minimum(qr + half, NA_QROWS)))
    tabs = []
    for shift, first in geoms:
        dr_idx = np.clip(kr - qr - shift + NA_WIN_R - 1, 0, 2 * NA_WIN_R - 2)
        row_ok = (kr >= first) & (kr < first + NA_WIN_R)
        bias = jnp.take(col_bias, jnp.asarray(dr_idx.reshape(-1)), axis=1)
        bias = bias.reshape(NA_HEADS, NA_QROWS, NA_KROWS, GRID_W, GRID_W).transpose(0, 1, 3, 2, 4)
        ok = row_ok[:, None, :, None] & col_ok[None, :, None, :]
        tabs.append(jnp.where(jnp.asarray(ok)[None], bias, NEG_BIG).reshape(NA_HEADS, NA_BQ, NA_BK))
    return jnp.stack(tabs)


def _natten_kernel(q_ref, k_ref, v_ref, tab_ref, o_ref, *, nblk):
    i = pl.program_id(2)
    krow0 = jnp.clip(NA_QROWS * i - NA_WIN_R // 2, 0, nblk * NA_QROWS - NA_KROWS)
    start = pl.multiple_of(krow0 * GRID_W, NA_WIN_R // 2 * GRID_W)
    for hh in range(NA_HPS):
        sl = slice(hh * HEAD_DIM, (hh + 1) * HEAD_DIM)
        kk = k_ref[pl.ds(start, NA_BK), sl]
        vv = v_ref[pl.ds(start, NA_BK), sl]
        s = lax.dot_general(q_ref[:, sl], kk, (((1,), (1,)), ((), ())), preferred_element_type=_F32) + tab_ref[hh]
        m = jnp.max(s, axis=-1, keepdims=True)
        p = jnp.exp(s - m)
        l = jnp.sum(p, axis=-1, keepdims=True)
        o = jnp.dot(p.astype(_BF16), vv, preferred_element_type=_F32)
        o_ref[:, sl] = (o / l).astype(o_ref.dtype)


def _natten_mixer(p_na, tables, *, batch, seq, b_off):
    nblk = seq // NA_BQ
    assert nblk >= 2 and seq % NA_BQ == 0
    p3 = p_na.reshape(-1, seq, 3 * NA_W)

    def variant(i):
        return jnp.where(i == 0, 0, jnp.where(i == nblk - 1, 2, 1))

    hgroups = NA_HEADS // NA_HPS
    width = NA_HPS * HEAD_DIM
    full = lambda part: pl.BlockSpec((None, seq, width), lambda b, h, i: (b + b_off, 0, part * hgroups + h))
    y = pl.pallas_call(
        functools.partial(_natten_kernel, nblk=nblk),
        grid=(batch, hgroups, nblk),
        in_specs=[pl.BlockSpec((None, NA_BQ, width), lambda b, h, i: (b + b_off, i, h)),
                  full(1), full(2),
                  pl.BlockSpec((None, NA_HPS, NA_BQ, NA_BK), lambda b, h, i: (variant(i), h, 0, 0))],
        out_specs=pl.BlockSpec((None, NA_BQ, width), lambda b, h, i: (b, i, h)),
        out_shape=jax.ShapeDtypeStruct((batch, seq, NA_W), _BF16),
        compiler_params=_cparams("parallel", "parallel", "arbitrary"),
        name="natten",
    )(p3, p3, p3, tables)
    return y.reshape(batch * seq, NA_W)


def _rope_tables(seq):
    pos = jnp.arange(seq, dtype=_F32)
    inv_freq = jnp.float32(ROPE_THETA) ** (-jnp.arange(0, ROPE_DIM, 2, dtype=_F32) / ROPE_DIM)
    ang = pos[:, None] * inv_freq[None, :]
    cos, sin = jnp.cos(ang), jnp.sin(ang)
    one = jnp.ones((seq, DF_HALF - ROPE_DIM), _F32)
    zero = jnp.zeros((seq, ROPE_HALF), _F32)
    zrest = jnp.zeros((seq, DF_HALF - ROPE_DIM), _F32)
    cos_h = jnp.concatenate([cos, cos, one], axis=1)
    up_h = jnp.concatenate([-sin, zero, zrest], axis=1)
    dn_h = jnp.concatenate([zero, sin, zrest], axis=1)
    two = lambda t: jnp.concatenate([t, t], axis=1)
    return two(cos_h), two(up_h), two(dn_h)


def _rope_kernel(t_ref, cos_ref, up_ref, dn_ref, *out_refs, split):
    cos, up, dn = cos_ref[...], up_ref[...], dn_ref[...]
    lane = lax.broadcasted_iota(jnp.int32, cos.shape, 1)
    for hd in range(t_ref.shape[1] // HEAD_DIM):
        sl = slice(hd * HEAD_DIM, (hd + 1) * HEAD_DIM)
        t = t_ref[:, sl]
        r = t * cos + pltpu.roll(t, HEAD_DIM - ROPE_HALF, 1) * up + pltpu.roll(t, ROPE_HALF, 1) * dn
        if split:
            out_refs[0][:, sl] = jnp.where(lane < DF_HALF, r, 0.0).astype(_BF16)
            out_refs[1][:, sl] = jnp.where(lane >= DF_HALF, r, 0.0).astype(_BF16)
        else:
            out_refs[0][:, sl] = r.astype(_BF16)


def _rope(p_dqk, tables, *, batch, seq, b_off, part, split):
    bt = min(ROPE_BT, seq)
    p3 = p_dqk.reshape(-1, seq, 2 * DF_W)
    tab = pl.BlockSpec((bt, HEAD_DIM), lambda b, i: (i, 0))
    out = pl.BlockSpec((None, bt, DF_W), lambda b, i: (b, i, 0))
    n_out = 2 if split else 1
    res = pl.pallas_call(
        functools.partial(_rope_kernel, split=split),
        grid=(batch, seq // bt),
        in_specs=[pl.BlockSpec((None, bt, DF_W), lambda b, i: (b + b_off, i, part)), tab, tab, tab],
        out_specs=[out] * n_out,
        out_shape=[jax.ShapeDtypeStruct((batch, seq, DF_W), _BF16)] * n_out,
        compiler_params=_cparams("parallel", "parallel"),
        name="rope_k" if split else "rope_q",
    )(p3, *tables)
    return res


def _diff_kernel(q_ref, ka_ref, kb_ref, vt_ref, lp_ref, ng_ref, o_ref, m_ref, acc_ref, sa_ref, sb_ref, *,
                 lambda_init, bk):
    nkb = ka_ref.shape[0] // bk
    q = q_ref[...]
    m_ref[...] = jnp.full_like(m_ref, -jnp.inf)
    acc_ref[...] = jnp.zeros_like(acc_ref)

    def scores(j, dst):
        start = pl.multiple_of(j * bk, bk)
        for mp, k_ref in enumerate((ka_ref, kb_ref)):
            dst[mp] = lax.dot_general(k_ref[pl.ds(start, bk), :], q, (((1,), (1,)), ((), ())),
                                      preferred_element_type=_F32)

    def accumulate(j, src):
        vt = vt_ref[j]
        for mp in range(2):
            s_t = src[mp]
            m_old = m_ref[mp]
            m_new = jnp.maximum(m_old, jnp.max(s_t, axis=0, keepdims=True))
            alpha = jnp.exp2(m_old - m_new)
            p_t = jnp.exp2(s_t - m_new).astype(_BF16)
            acc_ref[mp] = alpha * acc_ref[mp] + jnp.dot(vt, p_t, preferred_element_type=_F32)
            m_ref[mp] = m_new

    scores(0, sa_ref)

    def pair(jj, carry):
        j = 2 * jj
        scores(j + 1, sb_ref)
        accumulate(j, sa_ref)
        scores(j + 2, sa_ref)
        accumulate(j + 1, sb_ref)
        return carry

    lax.fori_loop(0, nkb // 2 - 1, pair, 0)
    scores(nkb - 1, sb_ref)
    accumulate(nkb - 2, sa_ref)
    accumulate(nkb - 1, sb_ref)

    lp = lp_ref[...]
    dots = jnp.sum(lp[0:1] * lp[1:2], axis=-1, keepdims=True), jnp.sum(lp[2:3] * lp[3:4], axis=-1, keepdims=True)
    lam = jnp.exp(dots[0]) - jnp.exp(dots[1]) + lambda_init
    a0, a1 = acc_ref[0], acc_ref[1]
    o_t = a0[:HEAD_DIM] / a0[HEAD_DIM:HEAD_DIM + 1] - lam * (a1[:HEAD_DIM] / a1[HEAD_DIM:HEAD_DIM + 1])
    o_t = o_t * lax.rsqrt(jnp.mean(o_t * o_t, axis=0, keepdims=True) + LN_EPS) * ng_ref[...] * (1.0 - lambda_init)
    o_ref[...] = o_t.T.astype(o_ref.dtype)


def _values_transposed(p_dv_rows, batch, seq, bk):
    nkb = seq // bk
    vt = p_dv_rows.reshape(batch, nkb, bk, DF_HEADS, HEAD_DIM).transpose(0, 3, 1, 4, 2)
    ones = jnp.ones((batch, DF_HEADS, nkb, 1, bk), vt.dtype)
    zeros = jnp.zeros((batch, DF_HEADS, nkb, DF_VT_ROWS - HEAD_DIM - 1, bk), vt.dtype)
    return jnp.concatenate([vt, ones, zeros], axis=3)


def _diff_mixer(q_rot, k_a, k_b, p_dv_rows, lam_p, norm_g, lambda_init, *, batch, seq):
    bq = min(DF_BQ, seq)
    bk = min(DF_BK, seq)
    nkb = seq // bk
    assert nkb % 2 == 0
    vt = _values_transposed(p_dv_rows, batch, seq, bk)
    full = pl.BlockSpec((None, seq, HEAD_DIM), lambda b, h, i: (b, 0, h))
    blk = pl.BlockSpec((None, bq, HEAD_DIM), lambda b, h, i: (b, i, h))
    y = pl.pallas_call(
        functools.partial(_diff_kernel, lambda_init=lambda_init, bk=bk),
        grid=(batch, DF_HEADS, seq // bq),
        in_specs=[blk, full, full,
                  pl.BlockSpec((None, None, nkb, DF_VT_ROWS, bk), lambda b, h, i: (b, h, 0, 0, 0)),
                  pl.BlockSpec((4, DF_HALF), lambda b, h, i: (0, 0)),
                  pl.BlockSpec((HEAD_DIM, 1), lambda b, h, i: (0, 0))],
        out_specs=blk,
        out_shape=jax.ShapeDtypeStruct((batch, seq, DF_W), _BF16),
        scratch_shapes=[pltpu.VMEM((2, 1, bq), _F32), pltpu.VMEM((2, DF_VT_ROWS, bq), _F32),
                        pltpu.VMEM((2, bk, bq), _F32), pltpu.VMEM((2, bk, bq), _F32)],
        compiler_params=_cparams("parallel", "parallel", "arbitrary"),
        name="diff_attention",
    )(q_rot, k_a, k_b, vt, lam_p.astype(_F32), norm_g.reshape(HEAD_DIM, 1).astype(_F32))
    return y.reshape(batch * seq, DF_W)


def _router_kernel(x_ref, w_ref, b_ref, idx_ref, gate_ref):
    logits = lax.dot_general(w_ref[...], x_ref[...], (((1,), (1,)), ((), ())),
                             preferred_element_type=_F32) + b_ref[...]
    mx = jnp.max(logits, axis=0, keepdims=True)
    ex = jnp.exp(logits - mx)
    probs = ex / jnp.sum(ex, axis=0, keepdims=True)
    e = [probs[j * N_GROUPS:(j + 1) * N_GROUPS] for j in range(EXPERTS_PER_GROUP)]
    hi01, lo01 = jnp.maximum(e[0], e[1]), jnp.minimum(e[0], e[1])
    hi23, lo23 = jnp.maximum(e[2], e[3]), jnp.minimum(e[2], e[3])
    gscore = jnp.maximum(hi01, hi23) + jnp.maximum(jnp.minimum(hi01, hi23), jnp.maximum(lo01, lo23))
    gid = lax.broadcasted_iota(jnp.int32, gscore.shape, 0).astype(_F32)
    best = jnp.min(jnp.where(gscore == jnp.max(gscore, axis=0, keepdims=True), gid, float(N_GROUPS)),
                   axis=0, keepdims=True)
    within = [jnp.sum(jnp.where(gid == best, ej, 0.0), axis=0, keepdims=True) for ej in e]

    def first_max(vals):
        top = functools.reduce(jnp.maximum, vals)
        pos = jnp.full(top.shape, float(EXPERTS_PER_GROUP), _F32)
        for j in reversed(range(EXPERTS_PER_GROUP)):
            pos = jnp.where(vals[j] == top, float(j), pos)
        return top, pos

    v1, i1 = first_max(within)
    v2, i2 = first_max([jnp.where(i1 == float(j), -1.0, within[j]) for j in range(EXPERTS_PER_GROUP)])
    base = best * EXPERTS_PER_GROUP
    idx_ref[0:1, :] = (base + i1).astype(jnp.int32)
    idx_ref[1:2, :] = (base + i2).astype(jnp.int32)
    tot = v1 + v2
    gate_ref[0:1, :] = v1 / tot
    gate_ref[1:2, :] = v2 / tot


def _route(xb, w_router, b_router):
    n, d = xb.shape
    tm = min(ROUTE_TM, n)
    perm = (jnp.arange(N_EXPERTS) % N_GROUPS) * EXPERTS_PER_GROUP + jnp.arange(N_EXPERTS) // N_GROUPS
    w_t = w_router.T[perm].astype(_BF16)
    b_c = b_router.astype(_F32)[perm].reshape(N_EXPERTS, 1)
    return pl.pallas_call(
        _router_kernel, grid=(n // tm,),
        in_specs=[pl.BlockSpec((tm, d), lambda i: (i, 0)),
                  pl.BlockSpec((N_EXPERTS, d), lambda i: (0, 0)),
                  pl.BlockSpec((N_EXPERTS, 1), lambda i: (0, 0))],
        out_specs=[pl.BlockSpec((TOP_K, tm), lambda i: (0, i))] * 2,
        out_shape=[jax.ShapeDtypeStruct((TOP_K, n), jnp.int32), jax.ShapeDtypeStruct((TOP_K, n), _F32)],
        compiler_params=_cparams("parallel"), name="router",
    )(xb, w_t, b_c)


def _expert_kernel(be_ref, nv_ref, x_ref, wg_ref, wu_ref, wd_ref, o_ref, acc_ref):
    blk, f = pl.program_id(0), pl.program_id(1)

    @pl.when(f == 0)
    def _():
        acc_ref[...] = jnp.zeros_like(acc_ref)

    @pl.when(blk < nv_ref[0])
    def _():
        x = x_ref[...]
        hg = jnp.dot(x, wg_ref[...].astype(_BF16), preferred_element_type=_F32)
        hu = jnp.dot(x, wu_ref[...].astype(_BF16), preferred_element_type=_F32)
        h = (hg * jax.nn.sigmoid(hg) * hu).astype(_BF16)
        acc_ref[...] += jnp.dot(h, wd_ref[...].astype(_BF16), preferred_element_type=_F32)

    @pl.when(f == pl.num_programs(1) - 1)
    def _():
        o_ref[...] = acc_ref[...].astype(o_ref.dtype)


def _experts(x_rows, block_expert, n_live, wg, wu, wd, *, layer, tm):
    p_rows, d = x_rows.shape
    nb = p_rows // tm
    ff = wg.shape[-1]
    tf = min(MOE_TF, ff)
    nf = ff // tf

    def fsel(b, f, nv):
        return jnp.where(b < nv[0], f, nf - 1)

    grid_spec = pltpu.PrefetchScalarGridSpec(
        num_scalar_prefetch=2,
        grid=(nb, nf),
        in_specs=[
            pl.BlockSpec((tm, d), lambda b, f, be, nv: (b, 0)),
            pl.BlockSpec((None, None, d, tf), lambda b, f, be, nv: (layer, be[b], 0, fsel(b, f, nv))),
            pl.BlockSpec((None, None, d, tf), lambda b, f, be, nv: (layer, be[b], 0, fsel(b, f, nv))),
            pl.BlockSpec((None, None, tf, d), lambda b, f, be, nv: (layer, be[b], fsel(b, f, nv), 0)),
        ],
        out_specs=pl.BlockSpec((tm, d), lambda b, f, be, nv: (b, 0)),
        scratch_shapes=[pltpu.VMEM((tm, d), _F32)],
    )
    return pl.pallas_call(
        _expert_kernel, grid_spec=grid_spec,
        out_shape=jax.ShapeDtypeStruct((p_rows, d), _BF16),
        compiler_params=_cparams("arbitrary", "arbitrary"), name="experts",
    )(block_expert, n_live, x_rows, wg, wu, wd)


def _dispatch_plan(idx, n_tokens, tm):
    a_tot = n_tokens * TOP_K
    nb = (a_tot + N_EXPERTS * (tm - 1) + tm - 1) // tm
    flat_e = idx.reshape(-1)
    chunk = min(PLAN_CHUNK, a_tot)
    assert a_tot % chunk == 0
    onehot = (flat_e[:, None] == jnp.arange(N_EXPERTS)[None, :]).astype(_F32).reshape(-1, chunk, N_EXPERTS)
    earlier = jnp.asarray(np.tril(np.ones((chunk, chunk), np.float32), -1))
    within = jnp.einsum('ij,cje->cie', earlier, onehot)
    totals = jnp.sum(onehot, axis=1)
    nchunks = totals.shape[0]
    prior = jnp.asarray(np.tril(np.ones((nchunks, nchunks), np.float32), -1))
    before = within + jnp.einsum('ij,je->ie', prior, totals)[:, None, :]
    rank = jnp.sum(before * onehot, axis=2).reshape(-1).astype(jnp.int32)
    counts = jnp.sum(totals, axis=0).astype(jnp.int32)
    padded = (counts + tm - 1) // tm * tm
    pend = jnp.cumsum(padded)
    poff = pend - padded
    dest = poff[flat_e] + rank
    tok = jnp.tile(jnp.arange(n_tokens, dtype=jnp.int32), TOP_K)
    row_tok = jnp.zeros((nb * tm,), jnp.int32).at[dest].set(tok)
    block_start = jnp.arange(nb, dtype=jnp.int32) * tm
    block_expert = jnp.minimum(jnp.sum((pend[None, :] <= block_start[:, None]).astype(jnp.int32), axis=1),
                               N_EXPERTS - 1)
    n_live = (pend[-1] // tm).astype(jnp.int32).reshape(1)
    return row_tok, dest.reshape(TOP_K, n_tokens), block_expert, n_live


def _moe_block(x, xb, w_router, b_router, wg, wu, wd, layer, ln_g, ln_b, alpha):
    n, d = x.shape
    tm = min(MOE_TM, n)
    idx, gates = _route(xb, w_router, b_router)
    row_tok, pos, block_expert, n_live = _dispatch_plan(idx, n, tm)
    yb = _experts(xb[row_tok], block_expert, n_live, wg, wu, wd, layer=layer, tm=tm)
    return _combine_layer_norm(x, yb[pos.reshape(-1)], gates.T, ln_g, ln_b, alpha)


def _split_w_in(w_in_l, b_in_l):
    g0 = 4 * ML_W
    g1 = g0 + N_GATE_COLS
    w_main = jnp.concatenate([w_in_l[:, :g0], w_in_l[:, g1:]], axis=1).astype(_BF16)
    b_main = jnp.concatenate([b_in_l[:g0], b_in_l[g1:]]).astype(_F32).reshape(1, -1)
    pad = LANES - N_GATE_COLS
    w_gate = jnp.pad(w_in_l[:, g0:g1], ((0, 0), (0, pad))).astype(_BF16)
    b_gate = jnp.pad(b_in_l[g0:g1], (0, pad)).astype(_F32).reshape(1, -1)
    return w_main, b_main, w_gate, b_gate


def _trunk_layers(x, geoms, depth, w_in, b_in, ml_f_bias, ml_norm_g, na_rpb, df_lambda, df_norm_g, w_out,
                  ln_g, ln_b, w_router, b_router, w_gate, w_up, w_down):
    n, d = x.shape
    alpha = (2 * depth) ** 0.25
    xb = x.astype(_BF16)
    off_ml, off_na, off_dqk, off_dv = 0, 4 * ML_W, 4 * ML_W + 3 * NA_W, 4 * ML_W + 3 * NA_W + 2 * DF_W
    na_scale = jnp.ones((off_dv + DF_W,), _F32).at[off_na:off_na + NA_W].set(HEAD_DIM ** -0.5).reshape(1, -1)
    q_scale = DF_HALF ** -0.5 * math.log2(math.e)
    rope_tabs = {}
    for _, seq in geoms:
        if seq not in rope_tabs:
            cos, up, dn = _rope_tables(seq)
            rope_tabs[seq] = ((cos * q_scale, up * q_scale, dn * q_scale), (cos, up, dn))

    wg_b, wu_b, wd_b = w_gate, w_up, w_down
    for l in range(depth):
        lambda_init = 0.8 - 0.6 * math.exp(-0.3 * l)
        w_main, b_main, w_g, b_g = _split_w_in(w_in[l], b_in[l])
        p_ml = _proj(xb, w_main, b_main, col_off=off_ml, ncols=4 * ML_W, out_dtype=_F32, name="proj_mlstm")
        p_na = _proj(xb, w_main, b_main, col_off=off_na, ncols=3 * NA_W, out_dtype=_BF16, scale=na_scale,
                     name="proj_natten")
        p_dqk = _proj(xb, w_main, b_main, col_off=off_dqk, ncols=2 * DF_W, out_dtype=_F32, name="proj_diff_qk")
        p_dv = _proj(xb, w_main, b_main, col_off=off_dv, ncols=DF_W, out_dtype=_BF16, name="proj_diff_v")
        p_g = _proj(xb, w_g, b_g, col_off=0, ncols=LANES, out_dtype=_F32, name="proj_gates")
        tables = _natten_tables(na_rpb[l])

        mixed = []
        tok0 = 0
        for batch, seq in geoms:
            b_off = tok0 // seq
            assert b_off * seq == tok0
            L = min(ML_CHUNK, seq)
            gts = p_g[tok0:tok0 + batch * seq, :N_GATE_COLS].reshape(batch, seq, 4, ML_HEADS)
            gts = gts.transpose(2, 0, 3, 1).reshape(4, batch, ML_HEADS, seq // L, 1, L)
            y_ml = _mlstm_mixer(p_ml, gts, ml_f_bias[l].astype(_F32), ml_norm_g[l].astype(_F32),
                                batch=batch, seq=seq, b_off=b_off)
            y_na = _natten_mixer(p_na, tables, batch=batch, seq=seq, b_off=b_off)
            tq, tk = rope_tabs[seq]
            (q_rot,) = _rope(p_dqk, tq, batch=batch, seq=seq, b_off=b_off, part=0, split=False)
            k_a, k_b = _rope(p_dqk, tk, batch=batch, seq=seq, b_off=b_off, part=1, split=True)
            y_df = _diff_mixer(q_rot, k_a, k_b, p_dv[tok0:tok0 + batch * seq], df_lambda[l], df_norm_g[l],
                               lambda_init, batch=batch, seq=seq)
            mixed.append(jnp.concatenate([y_ml, y_na, y_df], axis=1))
            tok0 += batch * seq
        y_mix = jnp.concatenate(mixed, axis=0) if len(mixed) > 1 else mixed[0]

        z = _proj(y_mix, w_out[l].astype(_BF16), None, col_off=0, ncols=d, out_dtype=_F32, res=x, res_alpha=alpha,
                  name="proj_out")
        x, xb = _layer_norm(z, ln_g[l, 0].astype(_F32), ln_b[l, 0].astype(_F32))
        x, xb = _moe_block(x, xb, w_router, b_router, wg_b, wu_b, wd_b, l,
                           ln_g[l, 1].astype(_F32), ln_b[l, 1].astype(_F32), alpha)
    return x


@jax.jit
def kernel(x_prompt, x_sample, w_in, b_in, ml_f_bias, ml_norm_g, na_rpb, df_lambda, df_norm_g, w_out, ln_g, ln_b,
           w_router, b_router, w_gate, w_up, w_down):
    d = x_prompt.shape[-1]
    geoms = [x_prompt.shape[:2], x_sample.shape[:2]]
    x = jnp.concatenate([x_prompt.reshape(-1, d), x_sample.reshape(-1, d)], axis=0)
    y = _trunk_layers(x, geoms, w_in.shape[0], w_in, b_in, ml_f_bias, ml_norm_g, na_rpb, df_lambda, df_norm_g,
                      w_out, ln_g, ln_b, w_router, b_router, w_gate, w_up, w_down)
    n0 = x_prompt.shape[0] * x_prompt.shape[1]
    return y[:n0].reshape(x_prompt.shape), y[n0:].reshape(x_sample.shape)
```
